```python
import math
import jax, jax.numpy as jnp
from jax import lax
import numpy as np

D_MODEL = 1024
BATCH = 8
SEQ = 2048
DEPTH = 1
DEC_BATCH = 128
DEC_SEQ = 1
PAST_LEN = 16384
PAGE_SIZE = 128

EPS = 1e-6
D_S5 = D_MODEL // 2
S5_GROUP_CH = 16
S5_GROUPS = D_S5 // S5_GROUP_CH
S5_STATE = 64
D_LRU = D_MODEL // 2
LRU_HEADS = 8
LRU_BW = D_LRU // LRU_HEADS
CONV_W = 4
LRU_C = 8.0
PEER_HEADS = 8
PEER_DQ = 256
PEER_DK = PEER_DQ // 2
N_KEYS = 128
N_EXPERTS = N_KEYS * N_KEYS
PEER_TOPK = 16
PEER_BLOCK = 128
D_IN = D_S5 + 2 * D_LRU + 2 * D_MODEL
SPLITS = (D_S5, D_S5 + D_LRU, D_S5 + 2 * D_LRU, D_S5 + 2 * D_LRU + D_MODEL)

kernel_name = "hybrid_s5_rglru_peer_step"


def rmsnorm(x, g):
    xf = x.astype(jnp.float32)
    y = xf * lax.rsqrt(jnp.mean(xf * xf, axis=-1, keepdims=True) + EPS)
    return (y * g.astype(jnp.float32)).astype(x.dtype)


def linear_scan(a, b, h0):
    b = b.at[:, 0].add(a[:, 0] * h0)

    def comb(l, r):
        al, bl = l
        ar, br = r
        return ar * al, ar * bl + br

    _, h = lax.associative_scan(comb, (a, b), axis=1)
    return h


def complex_linear_scan(a_re, a_im, b_re, b_im, h0_re, h0_im):
    b_re = b_re.at[:, 0].add(a_re[:, 0] * h0_re - a_im[:, 0] * h0_im)
    b_im = b_im.at[:, 0].add(a_re[:, 0] * h0_im + a_im[:, 0] * h0_re)

    def comb(l, r):
        alr, ali, blr, bli = l
        arr, ari, brr, bri = r
        return (arr * alr - ari * ali,
                arr * ali + ari * alr,
                arr * blr - ari * bli + brr,
                arr * bli + ari * blr + bri)

    _, _, h_re, h_im = lax.associative_scan(comb, (a_re, a_im, b_re, b_im), axis=1)
    return h_re, h_im


def s5_branch(u, h0_re, h0_im, lam_re, lam_im, log_dt, b_re, b_im, c_re, c_im, d_skip, w_glu, b_glu):
    f32 = jnp.float32
    Bt, L, _ = u.shape
    uf = u.astype(f32)
    ug = uf.reshape(Bt, L, S5_GROUPS, S5_GROUP_CH)
    dt = jnp.exp(log_dt.astype(f32))[:, None]
    lr, li = lam_re.astype(f32), lam_im.astype(f32)
    mag = jnp.exp(lr * dt)
    abar_re = mag * jnp.cos(li * dt)
    abar_im = mag * jnp.sin(li * dt)
    nr, ni = abar_re - 1.0, abar_im
    den = lr * lr + li * li
    f_re = (nr * lr + ni * li) / den
    f_im = (ni * lr - nr * li) / den
    br, bi = b_re.astype(f32), b_im.astype(f32)
    bb_re = f_re[..., None] * br - f_im[..., None] * bi
    bb_im = f_re[..., None] * bi + f_im[..., None] * br
    bu_re = jnp.einsum('blgh,gph->blgp', ug, bb_re)
    bu_im = jnp.einsum('blgh,gph->blgp', ug, bb_im)
    a_re = jnp.broadcast_to(abar_re, bu_re.shape)
    a_im = jnp.broadcast_to(abar_im, bu_im.shape)
    h_re, h_im = complex_linear_scan(a_re, a_im, bu_re, bu_im,
                                     h0_re.astype(f32), h0_im.astype(f32))
    y = (jnp.einsum('blgp,ghp->blgh', h_re, c_re.astype(f32))
         - jnp.einsum('blgp,ghp->blgh', h_im, c_im.astype(f32)))
    y = y.reshape(Bt, L, D_S5) + d_skip.astype(f32) * uf
    z = jax.nn.gelu(y)
    out = z * jax.nn.sigmoid(z @ w_glu.astype(f32) + b_glu.astype(f32))
    return out.astype(u.dtype), h_re[:, -1], h_im[:, -1]


def rglru_branch(xr, yg, conv_buf, h0, conv_w, conv_b, w_a, b_a, w_x, b_x, lam):
    f32 = jnp.float32
    Bt, L, _ = xr.shape
    xp = jnp.concatenate([conv_buf.astype(xr.dtype), xr], axis=1)
    xc = conv_b + sum(xp[:, k:k + L] * conv_w[k] for k in range(CONV_W))
    new_buf = xp[:, -(CONV_W - 1):]
    xh = xc.astype(f32).reshape(Bt, L, LRU_HEADS, LRU_BW)
    r = jax.nn.sigmoid(jnp.einsum('blhi,hij->blhj', xh, w_a.astype(f32)) + b_a.astype(f32))
    i = jax.nn.sigmoid(jnp.einsum('blhi,hij->blhj', xh, w_x.astype(f32)) + b_x.astype(f32))
    log_a = -LRU_C * r * jax.nn.softplus(-lam.astype(f32))
    a = jnp.exp(log_a)
    b = jnp.sqrt(-jnp.expm1(2.0 * log_a)) * (i * xh)
    h = linear_scan(a.reshape(Bt, L, D_LRU), b.reshape(Bt, L, D_LRU), h0.astype(f32))
    out = h * jax.nn.gelu(yg.astype(f32))
    return out.astype(xr.dtype), h[:, -1], new_buf


def peer(x, w_q, sub_keys, u_tab, v_tab):
    Bt, L, D = x.shape
    T = Bt * L
    pad = (-T) % PEER_BLOCK
    xt = jnp.pad(x.reshape(T, D), ((0, pad), (0, 0)))
    xb = xt.reshape(-1, PEER_BLOCK, D)

    def block(xc):
        q = (xc @ w_q).reshape(PEER_BLOCK, PEER_HEADS, 2, PEER_DK)
        s = jnp.einsum('chsk,hsnk->chsn', q, sub_keys).astype(jnp.float32)
        sv, si = lax.top_k(s, PEER_TOPK)
        cand = sv[:, :, 0, :, None] + sv[:, :, 1, None, :]
        cidx = si[:, :, 0, :, None] * N_KEYS + si[:, :, 1, None, :]
        cs, sel = lax.top_k(cand.reshape(PEER_BLOCK, PEER_HEADS, PEER_TOPK * PEER_TOPK), PEER_TOPK)
        eidx = jnp.take_along_axis(cidx.reshape(PEER_BLOCK, PEER_HEADS, PEER_TOPK * PEER_TOPK), sel, axis=-1)
        g = jax.nn.softmax(cs, axis=-1)
        u = u_tab[eidx]
        act = jax.nn.gelu(jnp.einsum('chkd,cd->chk', u, xc).astype(jnp.float32))
        w = (g * act).astype(xc.dtype)
        return jnp.einsum('chk,chkd->cd', w, v_tab[eidx])

    out = lax.map(block, xb).reshape(-1, D)[:T]
    return out.reshape(Bt, L, D).astype(x.dtype)


def layer(x, h0_re, h0_im, lru_h0, conv_buf,
          w_in, lam_re, lam_im, log_dt, b_re, b_im, c_re, c_im, d_skip, w_glu, b_glu,
          conv_w, conv_b, w_a, b_a, w_x, b_x, lam, w_proj_a, w_proj_b, w_out,
          norm_mix, norm_ffn, w_q, sub_keys, u_tab, v_tab):
    h = rmsnorm(x, norm_mix)
    proj = h @ w_in
    u_s5, x_lru, y_lru, g_a, g_b = jnp.split(proj, SPLITS, axis=-1)
    a_out, s5_re, s5_im = s5_branch(u_s5, h0_re, h0_im, lam_re, lam_im, log_dt,
                                    b_re, b_im, c_re, c_im, d_skip, w_glu, b_glu)
    b_out, lru_h, new_buf = rglru_branch(x_lru, y_lru, conv_buf, lru_h0,
                                         conv_w, conv_b, w_a, b_a, w_x, b_x, lam)
    merged = (jax.nn.sigmoid(g_a) * (a_out @ w_proj_a)
              + jax.nn.sigmoid(g_b) * (b_out @ w_proj_b))
    x = x + merged @ w_out
    x = x + peer(rmsnorm(x, norm_ffn), w_q, sub_keys, u_tab, v_tab)
    return x, (s5_re, s5_im, lru_h, new_buf)


def setup_inputs(seed: int = 0) -> dict:
    key = jax.random.key(seed)
    ks = iter(jax.random.split(key, 48))
    f32 = jnp.float32

    def nrm(shape, scale):
        return scale * jax.random.normal(next(ks), shape, f32)

    n = jnp.arange(S5_STATE, dtype=f32)
    a0 = jax.random.uniform(next(ks), (DEPTH, LRU_HEADS, LRU_BW), f32, 0.9, 0.999)
    base = a0 ** (1.0 / LRU_C)
    return {
        "x_prompt": nrm((BATCH, SEQ, D_MODEL), 1.0),
        "x_sample": nrm((DEC_BATCH, DEC_SEQ, D_MODEL), 1.0),
        "state_s5_re": nrm((DEPTH, DEC_BATCH, S5_GROUPS, S5_STATE), 0.3),
        "state_s5_im": nrm((DEPTH, DEC_BATCH, S5_GROUPS, S5_STATE), 0.3),
        "state_lru_h": nrm((DEPTH, DEC_BATCH, D_LRU), 0.5),
        "state_conv": nrm((DEPTH, DEC_BATCH, CONV_W - 1, D_LRU), 1.0),
        "w_in": nrm((DEPTH, D_MODEL, D_IN), D_MODEL ** -0.5),
        "s5_lam_re": -0.5 + nrm((DEPTH, S5_GROUPS, S5_STATE), 0.01),
        "s5_lam_im": math.pi * n + nrm((DEPTH, S5_GROUPS, S5_STATE), 0.01),
        "s5_log_dt": jax.random.uniform(next(ks), (DEPTH, S5_GROUPS), f32, math.log(1e-3), math.log(1e-1)),
        "s5_b_re": nrm((DEPTH, S5_GROUPS, S5_STATE, S5_GROUP_CH), (2 * S5_GROUP_CH) ** -0.5),
        "s5_b_im": nrm((DEPTH, S5_GROUPS, S5_STATE, S5_GROUP_CH), (2 * S5_GROUP_CH) ** -0.5),
        "s5_c_re": nrm((DEPTH, S5_GROUPS, S5_GROUP_CH, S5_STATE), (2 * S5_STATE) ** -0.5),
        "s5_c_im": nrm((DEPTH, S5_GROUPS, S5_GROUP_CH, S5_STATE), (2 * S5_STATE) ** -0.5),
        "s5_d": 1.0 + nrm((DEPTH, D_S5), 0.1),
        "s5_w_glu": nrm((DEPTH, D_S5, D_S5), D_S5 ** -0.5),
        "s5_b_glu": nrm((DEPTH, D_S5), 0.01),
        "conv_w": nrm((DEPTH, CONV_W, D_LRU), CONV_W ** -0.5),
        "conv_b": nrm((DEPTH, D_LRU), 0.01),
        "lru_w_a": nrm((DEPTH, LRU_HEADS, LRU_BW, LRU_BW), LRU_BW ** -0.5),
        "lru_b_a": nrm((DEPTH, LRU_HEADS, LRU_BW), 0.01),
        "lru_w_x": nrm((DEPTH, LRU_HEADS, LRU_BW, LRU_BW), LRU_BW ** -0.5),
        "lru_b_x": nrm((DEPTH, LRU_HEADS, LRU_BW), 0.01),
        "lru_lam": jnp.log(base) - jnp.log1p(-base),
        "w_proj_a": nrm((DEPTH, D_S5, D_MODEL), D_S5 ** -0.5),
        "w_proj_b": nrm((DEPTH, D_LRU, D_MODEL), D_LRU ** -0.5),
        "w_out": nrm((DEPTH, D_MODEL, D_MODEL), D_MODEL ** -0.5),
        "norm_mix": 1.0 + nrm((DEPTH, D_MODEL), 0.05),
        "norm_ffn": 1.0 + nrm((DEPTH, D_MODEL), 0.05),
        "peer_w_q": nrm((DEPTH, D_MODEL, PEER_HEADS * PEER_DQ), D_MODEL ** -0.5),
        "peer_keys": nrm((DEPTH, PEER_HEADS, 2, N_KEYS, PEER_DK), PEER_DK ** -0.5),
        "peer_u": nrm((DEPTH, N_EXPERTS, D_MODEL), D_MODEL ** -0.5),
        "peer_v": nrm((DEPTH, N_EXPERTS, D_MODEL), PEER_HEADS ** -0.5),
        "norm_final": 1.0 + nrm((D_MODEL,), 0.05),
    }


def reference(x_prompt, x_sample, state_s5_re, state_s5_im, state_lru_h, state_conv,
              w_in, s5_lam_re, s5_lam_im, s5_log_dt, s5_b_re, s5_b_im, s5_c_re, s5_c_im,
              s5_d, s5_w_glu, s5_b_glu, conv_w, conv_b, lru_w_a, lru_b_a, lru_w_x, lru_b_x,
              lru_lam, w_proj_a, w_proj_b, w_out, norm_mix, norm_ffn,
              peer_w_q, peer_keys, peer_u, peer_v, norm_final):
    xp, xs = x_prompt, x_sample
    Bp = xp.shape[0]
    zp_re = jnp.zeros((Bp, S5_GROUPS, S5_STATE), jnp.float32)
    zp_im = jnp.zeros((Bp, S5_GROUPS, S5_STATE), jnp.float32)
    zp_h = jnp.zeros((Bp, D_LRU), jnp.float32)
    zp_conv = jnp.zeros((Bp, CONV_W - 1, D_LRU), xp.dtype)
    new_p = [[], [], [], []]
    new_s = [[], [], [], []]
    for l in range(DEPTH):
        params = (w_in[l], s5_lam_re[l], s5_lam_im[l], s5_log_dt[l], s5_b_re[l], s5_b_im[l],
                  s5_c_re[l], s5_c_im[l], s5_d[l], s5_w_glu[l], s5_b_glu[l],
                  conv_w[l], conv_b[l], lru_w_a[l], lru_b_a[l], lru_w_x[l], lru_b_x[l], lru_lam[l],
                  w_proj_a[l], w_proj_b[l], w_out[l], norm_mix[l], norm_ffn[l],
                  peer_w_q[l], peer_keys[l], peer_u[l], peer_v[l])
        xp, st_p = layer(xp, zp_re, zp_im, zp_h, zp_conv, *params)
        xs, st_s = layer(xs, state_s5_re[l], state_s5_im[l], state_lru_h[l], state_conv[l], *params)
        for j in range(4):
            new_p[j].append(st_p[j].astype(xp.dtype))
            new_s[j].append(st_s[j].astype(xs.dtype))
    y_prompt = rmsnorm(xp, norm_final)
    y_sample = rmsnorm(xs, norm_final)
    s5_re_p = jnp.stack(new_p[0])
    s5_im_p = jnp.stack(new_p[1])
    lru_h_p = jnp.stack(new_p[2])
    conv_p = jnp.stack(new_p[3])
    s5_re_s = jnp.stack(new_s[0])
    s5_im_s = jnp.stack(new_s[1])
    lru_h_s = jnp.stack(new_s[2])
    conv_s = jnp.stack(new_s[3])
    return (y_prompt, y_sample, s5_re_p, s5_im_p, lru_h_p, conv_p, s5_re_s, s5_im_s, lru_h_s, conv_s)
```

```python
import functools
import math

import jax
import jax.numpy as jnp
from jax import lax
from jax.experimental import pallas as pl
from jax.experimental.pallas import tpu as pltpu

F32 = jnp.float32
BF16 = jnp.bfloat16

EPS = 1e-6
LRU_C = 8.0
CONV_W = 4
LANES = 128
SUBLANES = 8
BF16_ROWS = 16
TOPK = 16
N_EXTRACT = TOPK + 1
VAL_ROWS = 24
VMEM_LIMIT = 56 * 1024 * 1024


def _gelu(x):
    c = math.sqrt(2.0 / math.pi)
    return x * (0.5 * (1.0 + jnp.tanh(c * (x + 0.044715 * (x * x * x)))))


def _sigmoid(x):
    return 1.0 / (1.0 + jnp.exp(-x))


def _rms(x, g):
    return x * lax.rsqrt(jnp.mean(x * x, axis=-1, keepdims=True) + EPS) * g


def _dot(a, b):
    return jnp.dot(a.astype(BF16), b, preferred_element_type=F32)


def _const_spec(shape):
    nd = len(shape)
    return pl.BlockSpec(shape, lambda *_: (0,) * nd, pipeline_mode=pl.Buffered(1))


def _s5_disc_body(lr_ref, li_ref, ldt_ref, br_ref, bi_ref,
                  are_ref, aim_ref, bbr_ref, bbi_ref):
    lr = lr_ref[...]
    li = li_ref[...]
    dt = jnp.exp(ldt_ref[...])
    mag = jnp.exp(lr * dt)
    a_re = mag * jnp.cos(li * dt)
    a_im = mag * jnp.sin(li * dt)
    nr = a_re - 1.0
    ni = a_im
    den = lr * lr + li * li
    f_re = (nr * lr + ni * li) / den
    f_im = (ni * lr - nr * li) / den
    br = br_ref[...]
    bi = bi_ref[...]
    are_ref[...] = a_re
    aim_ref[...] = a_im
    bbr_ref[...] = f_re * br - f_im * bi
    bbi_ref[...] = f_re * bi + f_im * br


def _s5_discretise(lam_re, lam_im, log_dt, b_re, b_im):
    g, p, h = b_re.shape
    rep = lambda a: jnp.repeat(a, h, axis=0)
    tr = lambda b: jnp.transpose(b, (0, 2, 1)).reshape(g * h, p)
    shp = jax.ShapeDtypeStruct((g * h, p), F32)
    return pl.pallas_call(
        _s5_disc_body,
        out_shape=(shp, shp, shp, shp),
        name="s5_discretise",
    )(rep(lam_re), rep(lam_im), rep(log_dt[:, None]), tr(b_re), tr(b_im))


def _block_diag(blocks):
    g, r, c = blocks.shape
    eye = jnp.eye(g, dtype=blocks.dtype)
    return (blocks[:, :, None, :] * eye[:, None, :, None]).reshape(g * r, g * c)


def _mixer_body(x_ref, s5re0, s5im0, lruh0, conv0,
                nmix, win, bmat, abre, abim, cre, cim, dskip, wglu, bglu,
                convw, convb, wax, bax, lam, wpa, wpb, wout,
                x1_ref, s5re, s5im, lruh, convo,
                proj, bu, la, lb, xbuf, *, nb, tl, cw):
    rows = nb * tl
    d_s5 = dskip.shape[-1]
    d_lru = lam.shape[-1]
    d_model = x_ref.shape[-1]
    n_state = abre.shape[-1]
    tail = (CONV_W - 1) * nb

    @pl.when(pl.program_id(0) == 0)
    def _():
        s5re[...] = s5re0[...]
        s5im[...] = s5im0[...]
        lruh[...] = lruh0[...]
        xbuf[0:tail, :] = conv0[...]

    x = x_ref[...]
    proj[...] = _dot(_rms(x, nmix[...]), win[...])
    c0, c1, c2, c3 = d_s5, d_s5 + d_lru, d_s5 + 2 * d_lru, d_s5 + 2 * d_lru + d_model

    u = proj[:, 0:c0]
    bu[...] = _dot(u, bmat[...])
    for c in range(n_state // cw):
        lo = c * cw
        ar = jnp.broadcast_to(abre[:, lo:lo + cw], (nb, cw))
        ai = jnp.broadcast_to(abim[:, lo:lo + cw], (nb, cw))

        def s5_step(t, carry, lo=lo, ar=ar, ai=ai):
            hr, hi = carry
            r0 = pl.multiple_of(t * nb, nb)
            br = bu[pl.ds(r0, nb), lo:lo + cw]
            bi = bu[pl.ds(r0, nb), n_state + lo:n_state + lo + cw]
            nr = ar * hr - ai * hi + br
            ni = ar * hi + ai * hr + bi
            bu[pl.ds(r0, nb), lo:lo + cw] = nr
            bu[pl.ds(r0, nb), n_state + lo:n_state + lo + cw] = ni
            return nr, ni

        hr, hi = lax.fori_loop(0, tl, s5_step, (s5re[:, lo:lo + cw], s5im[:, lo:lo + cw]),
                               unroll=min(tl, 8))
        s5re[:, lo:lo + cw] = hr
        s5im[:, lo:lo + cw] = hi

    y = _dot(bu[:, 0:n_state], cre[...]) - _dot(bu[:, n_state:2 * n_state], cim[...])
    y = y + dskip[...] * u
    z = _gelu(y)
    out_a = z * _sigmoid(_dot(z, wglu[...]) + bglu[...])

    xbuf[tail:tail + rows, :] = proj[:, c0:c1]
    acc = xbuf[0:rows, :] * convw[0:1, :]
    for k in range(1, CONV_W):
        acc = acc + xbuf[k * nb:k * nb + rows, :] * convw[k:k + 1, :]
    xc = convb[...] + acc
    new_tail = xbuf[tl * nb:tl * nb + tail, :]
    convo[...] = new_tail
    xbuf[0:tail, :] = new_tail

    gates = _dot(xc, wax[...]) + bax[...]
    r_gate = _sigmoid(gates[:, 0:d_lru])
    i_gate = _sigmoid(gates[:, d_lru:2 * d_lru])
    neg_lam = -lam[...]
    softplus = jnp.maximum(neg_lam, 0.0) + jnp.log1p(jnp.exp(-jnp.abs(neg_lam)))
    log_a = (-LRU_C * r_gate) * softplus
    a_t = jnp.exp(log_a)
    la[...] = a_t
    lb[...] = jnp.sqrt(-jnp.tanh(log_a) * (a_t * a_t + 1.0)) * (i_gate * xc)

    def lru_step(t, h):
        r0 = pl.multiple_of(t * nb, nb)
        hn = la[pl.ds(r0, nb), :] * h + lb[pl.ds(r0, nb), :]
        lb[pl.ds(r0, nb), :] = hn
        return hn

    lruh[...] = lax.fori_loop(0, tl, lru_step, lruh[...], unroll=min(tl, 8))
    out_b = lb[...] * _gelu(proj[:, c1:c2])

    merged = (_sigmoid(proj[:, c2:c3]) * _dot(out_a, wpa[...])
              + _sigmoid(proj[:, c3:c3 + d_model]) * _dot(out_b, wpb[...]))
    x1_ref[...] = x + _dot(merged, wout[...])


def _mixer(x_tm, s5re0, s5im0, lruh0, conv0, w, *, nb, tl, cw):
    rows_total, d_model = x_tm.shape
    rows = nb * tl
    n_state = s5re0.shape[-1]
    d_lru = lruh0.shape[-1]
    d_in = w["win"].shape[-1]
    tail = (CONV_W - 1) * nb
    weights = (w["nmix"], w["win"], w["bmat"], w["abre"], w["abim"], w["cre"], w["cim"],
               w["dskip"], w["wglu"], w["bglu"], w["convw"], w["convb"], w["wax"], w["bax"],
               w["lam"], w["wpa"], w["wpb"], w["wout"])
    states = (s5re0, s5im0, lruh0, conv0)
    row_spec = pl.BlockSpec((rows, d_model), lambda i: (i, 0))
    return pl.pallas_call(
        functools.partial(_mixer_body, nb=nb, tl=tl, cw=cw),
        grid=(rows_total // rows,),
        in_specs=[row_spec] + [_const_spec(a.shape) for a in states + weights],
        out_specs=[row_spec] + [_const_spec(a.shape) for a in states],
        out_shape=[jax.ShapeDtypeStruct(x_tm.shape, F32)]
        + [jax.ShapeDtypeStruct(a.shape, F32) for a in states],
        scratch_shapes=[
            pltpu.VMEM((rows, d_in), F32),
            pltpu.VMEM((rows, 2 * n_state), F32),
            pltpu.VMEM((rows, d_lru), F32),
            pltpu.VMEM((rows, d_lru), F32),
            pltpu.VMEM((rows + tail, d_lru), F32),
        ],
        compiler_params=pltpu.CompilerParams(
            dimension_semantics=("arbitrary",), vmem_limit_bytes=VMEM_LIMIT),
        name="mixer",
    )(x_tm, *states, *weights)


def _peer_select_body(x1_ref, nffn, wq, keys, xn_ref, e1_ref, e2_ref, kap_ref,
                      vals, *, tt, heads, n_keys):
    neg_inf = -jnp.inf
    xn = _rms(x1_ref[...], nffn[...]).astype(BF16)
    xn_ref[...] = xn
    q = jnp.dot(xn, wq[...], preferred_element_type=F32).astype(BF16)
    dk = keys.shape[-1]
    rowv = lax.broadcasted_iota(jnp.int32, (VAL_ROWS, LANES), 0)

    def extract(s_init):
        def rnd(k, carry):
            s, out = carry
            m = jnp.max(s, axis=0, keepdims=True)
            return jnp.where(s >= m, neg_inf, s), jnp.where(rowv == k, m, out)
        _, out = lax.fori_loop(0, N_EXTRACT, rnd, (s_init, jnp.full((VAL_ROWS, LANES), neg_inf, F32)))
        return out

    for g in range(2 * heads):
        s_t = lax.dot_general(keys[g], q[:, g * dk:(g + 1) * dk], (((1,), (1,)), ((), ())),
                              preferred_element_type=F32)
        dst = e1_ref if g % 2 == 0 else e2_ref
        h = g // 2
        dst[h * n_keys:(h + 1) * n_keys, :] = s_t
        for c in range(tt // LANES):
            cs = slice(c * LANES, (c + 1) * LANES)
            vals[g, :, cs] = extract(dst[h * n_keys:(h + 1) * n_keys, cs])

    row8 = lax.broadcasted_iota(jnp.int32, (SUBLANES, LANES), 0)
    for h in range(heads):
        for c in range(tt // LANES):
            cs = slice(c * LANES, (c + 1) * LANES)
            a1 = vals[2 * h, :, cs]
            a2 = vals[2 * h + 1, :, cs]
            cand = [a1[0:1, :] + a2]
            for p in range(1, N_EXTRACT):
                nq = N_EXTRACT // (p + 1)
                cand.append(jnp.where(row8 < nq, a1[p:p + 1, :] + a2[0:SUBLANES, :], neg_inf))
            cand = jnp.concatenate(cand, axis=0)
            cv = extract(cand)
            top = cv[0:TOPK, :]
            v1 = top[0:1, :]
            zsum = jnp.sum(jnp.exp(top - v1), axis=0, keepdims=True)
            thr = 0.5 * (cv[TOPK - 1:TOPK, :] + cv[TOPK:TOPK + 1, :])
            kap_ref[h:h + 1, cs] = jnp.exp(thr - v1) / zsum
            hs = slice(h * n_keys, (h + 1) * n_keys)
            s1 = e1_ref[hs, cs]
            s2 = e2_ref[hs, cs]
            e1_ref[hs, cs] = jnp.where(s1 >= a1[TOPK - 1:TOPK, :], jnp.exp(s1 - a1[0:1, :]), 0.0) / zsum
            e2_ref[hs, cs] = jnp.where(s2 >= a2[TOPK - 1:TOPK, :], jnp.exp(s2 - a2[0:1, :]), 0.0)


def _peer_select(x1, nffn, wq, keys, *, tt):
    t_total, d_model = x1.shape
    groups, n_keys, _ = keys.shape
    heads = groups // 2
    return pl.pallas_call(
        functools.partial(_peer_select_body, tt=tt, heads=heads, n_keys=n_keys),
        grid=(t_total // tt,),
        in_specs=[pl.BlockSpec((tt, d_model), lambda i: (i, 0)),
                  _const_spec(nffn.shape), _const_spec(wq.shape), _const_spec(keys.shape)],
        out_specs=[pl.BlockSpec((tt, d_model), lambda i: (i, 0)),
                   pl.BlockSpec((heads * n_keys, tt), lambda i: (0, i)),
                   pl.BlockSpec((heads * n_keys, tt), lambda i: (0, i)),
                   pl.BlockSpec((heads, tt), lambda i: (0, i))],
        out_shape=[jax.ShapeDtypeStruct((t_total, d_model), BF16),
                   jax.ShapeDtypeStruct((heads * n_keys, t_total), F32),
                   jax.ShapeDtypeStruct((heads * n_keys, t_total), F32),
                   jax.ShapeDtypeStruct((heads, t_total), F32)],
        scratch_shapes=[pltpu.VMEM((groups, VAL_ROWS, tt), F32)],
        compiler_params=pltpu.CompilerParams(
            dimension_semantics=("parallel",), vmem_limit_bytes=VMEM_LIMIT),
        name="peer_select",
    )(x1, nffn, wq, keys)


def _peer_dense_body(xn_ref, x1_ref, e1_ref, e2_ref, kap_ref, u_ref, v_ref, nfin,
                     y_ref, acc, act, wgt, *, tt, ni, heads, n_keys):
    e = pl.program_id(1)

    @pl.when(e == 0)
    def _():
        acc[...] = jnp.zeros_like(acc)

    act[...] = lax.dot_general(u_ref[...], xn_ref[...], (((1,), (1,)), ((), ())),
                               preferred_element_type=F32)
    assert ni == SUBLANES
    for il in range(ni):
        for c in range(tt // LANES):
            cs = slice(c * LANES, (c + 1) * LANES)
            e1b = [jnp.broadcast_to(
                e1_ref[pl.ds(pl.multiple_of(h * n_keys + e * ni, ni), ni), cs][il:il + 1, :],
                (BF16_ROWS, LANES)) for h in range(heads)]
            kb = [jnp.broadcast_to(kap_ref[h:h + 1, cs], (BF16_ROWS, LANES)) for h in range(heads)]

            def blk(rb, carry, il=il, cs=cs, e1b=e1b, kb=kb):
                r0 = pl.multiple_of(rb * BF16_ROWS, BF16_ROWS)
                g = jnp.zeros((BF16_ROWS, LANES), F32)
                for h in range(heads):
                    p = e1b[h] * e2_ref[pl.ds(h * n_keys + r0, BF16_ROWS), cs]
                    g = g + jnp.where(p >= kb[h], p, 0.0)
                a = act[pl.ds(il * n_keys + r0, BF16_ROWS), cs]
                wgt[pl.ds(il * n_keys + r0, BF16_ROWS), cs] = (g * _gelu(a)).astype(BF16)
                return carry

            lax.fori_loop(0, n_keys // BF16_ROWS, blk, 0, unroll=2)

    acc[...] += lax.dot_general(wgt[...], v_ref[...], (((0,), (0,)), ((), ())),
                                preferred_element_type=F32)

    @pl.when(e == pl.num_programs(1) - 1)
    def _():
        y_ref[...] = _rms(x1_ref[...] + acc[...], nfin[...])


def _peer_dense(xn, x1, e1, e2, kap, u_tab, v_tab, nfin, *, tt, ni):
    t_total, d_model = x1.shape
    heads = kap.shape[0]
    n_keys = e1.shape[0] // heads
    te = ni * n_keys
    n_experts = u_tab.shape[0]
    tok = lambda shape: pl.BlockSpec(shape, lambda t, e: (t, 0))
    fac = lambda rows: pl.BlockSpec((rows, tt), lambda t, e: (0, t))
    tab = pl.BlockSpec((te, d_model), lambda t, e: (e, 0))
    return pl.pallas_call(
        functools.partial(_peer_dense_body, tt=tt, ni=ni, heads=heads, n_keys=n_keys),
        grid=(t_total // tt, n_experts // te),
        in_specs=[tok((tt, d_model)), tok((tt, d_model)), fac(heads * n_keys), fac(heads * n_keys),
                  fac(heads), tab, tab, pl.BlockSpec(nfin.shape, lambda t, e: (0, 0))],
        out_specs=tok((tt, d_model)),
        out_shape=jax.ShapeDtypeStruct((t_total, d_model), F32),
        scratch_shapes=[pltpu.VMEM((tt, d_model), F32),
                        pltpu.VMEM((te, tt), F32),
                        pltpu.VMEM((te, tt), BF16)],
        compiler_params=pltpu.CompilerParams(
            dimension_semantics=("parallel", "arbitrary"), vmem_limit_bytes=VMEM_LIMIT),
        name="peer_dense",
    )(xn, x1, e1, e2, kap, u_tab, v_tab, nfin)


def _layer(x_tm, s5re0, s5im0, lruh0, conv0, w, *, nb, tl, cw, tt, ni):
    x1, s5re, s5im, lruh, convo = _mixer(x_tm, s5re0, s5im0, lruh0, conv0, w, nb=nb, tl=tl, cw=cw)
    xn, e1, e2, kap = _peer_select(x1, w["nffn"], w["wq"], w["keys"], tt=tt)
    y = _peer_dense(xn, x1, e1, e2, kap, w["u"], w["v"], w["nfin"], tt=tt, ni=ni)
    return y, s5re, s5im, lruh, convo


def kernel(x_prompt, x_sample, state_s5_re, state_s5_im, state_lru_h, state_conv, w_in, s5_lam_re, s5_lam_im, s5_log_dt, s5_b_re, s5_b_im, s5_c_re, s5_c_im, s5_d, s5_w_glu, s5_b_glu, conv_w, conv_b, lru_w_a, lru_b_a, lru_w_x, lru_b_x, lru_lam, w_proj_a, w_proj_b, w_out, norm_mix, norm_ffn, peer_w_q, peer_keys, peer_u, peer_v, norm_final):
    depth = w_in.shape[0]
    assert depth == 1, "single trunk layer"
    bp, seq, d_model = x_prompt.shape
    bs, dec_seq, _ = x_sample.shape
    assert dec_seq == 1
    groups, n_state_g = s5_lam_re.shape[1:]
    g_ch = s5_b_re.shape[-1]
    d_lru = lru_lam.shape[1] * lru_lam.shape[2]
    heads, _, n_keys, dk = peer_keys.shape[1:]
    n_state = groups * n_state_g
    row = lambda a: a.reshape(1, -1).astype(F32)

    abar_re, abar_im, bb_re, bb_im = _s5_discretise(
        s5_lam_re[0], s5_lam_im[0], s5_log_dt[0], s5_b_re[0], s5_b_im[0])
    bb = lambda a: _block_diag(a.reshape(groups, g_ch, n_state_g))
    cmat = lambda c: _block_diag(jnp.transpose(c, (0, 2, 1)))
    w = dict(
        nmix=row(norm_mix[0]), win=w_in[0].astype(BF16),
        bmat=jnp.concatenate([bb(bb_re), bb(bb_im)], axis=1).astype(BF16),
        abre=abar_re[::g_ch].reshape(1, n_state), abim=abar_im[::g_ch].reshape(1, n_state),
        cre=cmat(s5_c_re[0]).astype(BF16), cim=cmat(s5_c_im[0]).astype(BF16),
        dskip=row(s5_d[0]), wglu=s5_w_glu[0].astype(BF16), bglu=row(s5_b_glu[0]),
        convw=conv_w[0], convb=row(conv_b[0]),
        wax=jnp.concatenate([_block_diag(lru_w_a[0]), _block_diag(lru_w_x[0])], axis=1).astype(BF16),
        bax=jnp.concatenate([row(lru_b_a[0]), row(lru_b_x[0])], axis=1),
        lam=row(lru_lam[0]), wpa=w_proj_a[0].astype(BF16), wpb=w_proj_b[0].astype(BF16),
        wout=w_out[0].astype(BF16), nffn=row(norm_ffn[0]), wq=peer_w_q[0].astype(BF16),
        keys=peer_keys[0].reshape(2 * heads, n_keys, dk).astype(BF16),
        u=peer_u[0].astype(BF16), v=peer_v[0].astype(BF16), nfin=row(norm_final),
    )

    xp_tm = jnp.transpose(x_prompt, (1, 0, 2)).reshape(seq * bp, d_model)
    zeros = lambda *s: jnp.zeros(s, F32)
    yp, p_re, p_im, p_h, p_conv = _layer(
        xp_tm, zeros(bp, n_state), zeros(bp, n_state), zeros(bp, d_lru),
        zeros((CONV_W - 1) * bp, d_lru), w, nb=bp, tl=32, cw=4 * LANES, tt=512, ni=SUBLANES)
    y_prompt = jnp.transpose(yp.reshape(seq, bp, d_model), (1, 0, 2))

    conv_tm = jnp.transpose(state_conv[0], (1, 0, 2)).reshape((CONV_W - 1) * bs, d_lru)
    ys, s_re, s_im, s_h, s_conv = _layer(
        x_sample.reshape(bs, d_model), state_s5_re[0].reshape(bs, n_state),
        state_s5_im[0].reshape(bs, n_state), state_lru_h[0], conv_tm, w,
        nb=bs, tl=1, cw=LANES, tt=LANES, ni=SUBLANES)
    y_sample = ys.reshape(bs, 1, d_model)

    st = lambda a, b: a.reshape(1, b, groups, n_state_g)
    cv = lambda a, b: jnp.transpose(a.reshape(CONV_W - 1, b, d_lru), (1, 0, 2))[None]
    return (y_prompt, y_sample,
            st(p_re, bp), st(p_im, bp), p_h[None], cv(p_conv, bp),
            st(s_re, bs), st(s_im, bs), s_h[None], cv(s_conv, bs))
```

```python
import functools
import math

import jax
import jax.numpy as jnp
from jax import lax
from jax.experimental import pallas as pl
from jax.experimental.pallas import tpu as pltpu

F32 = jnp.float32
BF16 = jnp.bfloat16

EPS = 1e-6
LRU_C = 8.0
CONV_W = 4
LANES = 128
SUBLANES = 8
BF16_ROWS = 16
TOPK = 16
N_EXTRACT = TOPK + 1
VAL_ROWS = 24
VMEM_LIMIT = 56 * 1024 * 1024


def _gelu(x):
    c = math.sqrt(2.0 / math.pi)
    return x * (0.5 * (1.0 + jnp.tanh(c * (x + 0.044715 * (x * x * x)))))


def _gelu_lowp(x):
    c = math.sqrt(2.0 / math.pi)
    return x * (0.5 + 0.5 * jnp.tanh(x * (c + (c * 0.044715) * (x * x))))


def _sigmoid(x):
    return 1.0 / (1.0 + jnp.exp(-x))


def _rms(x, g):
    return x * lax.rsqrt(jnp.mean(x * x, axis=-1, keepdims=True) + EPS) * g


def _dot(a, b):
    return jnp.dot(a.astype(BF16), b, preferred_element_type=F32)


def _const_spec(shape):
    nd = len(shape)
    return pl.BlockSpec(shape, lambda *_: (0,) * nd, pipeline_mode=pl.Buffered(1))


def _s5_disc_body(lr_ref, li_ref, ldt_ref, br_ref, bi_ref,
                  are_ref, aim_ref, bbr_ref, bbi_ref):
    lr = lr_ref[...]
    li = li_ref[...]
    dt = jnp.exp(ldt_ref[...])
    mag = jnp.exp(lr * dt)
    a_re = mag * jnp.cos(li * dt)
    a_im = mag * jnp.sin(li * dt)
    nr = a_re - 1.0
    ni = a_im
    den = lr * lr + li * li
    f_re = (nr * lr + ni * li) / den
    f_im = (ni * lr - nr * li) / den
    br = br_ref[...]
    bi = bi_ref[...]
    are_ref[...] = a_re
    aim_ref[...] = a_im
    bbr_ref[...] = f_re * br - f_im * bi
    bbi_ref[...] = f_re * bi + f_im * br


def _s5_discretise(lam_re, lam_im, log_dt, b_re, b_im):
    g, p, h = b_re.shape
    rep = lambda a: jnp.repeat(a, h, axis=0)
    tr = lambda b: jnp.transpose(b, (0, 2, 1)).reshape(g * h, p)
    shp = jax.ShapeDtypeStruct((g * h, p), F32)
    return pl.pallas_call(
        _s5_disc_body,
        out_shape=(shp, shp, shp, shp),
        name="s5_discretise",
    )(rep(lam_re), rep(lam_im), rep(log_dt[:, None]), tr(b_re), tr(b_im))


def _block_diag(blocks):
    g, r, c = blocks.shape
    eye = jnp.eye(g, dtype=blocks.dtype)
    return (blocks[:, :, None, :] * eye[:, None, :, None]).reshape(g * r, g * c)


def _mixer_body(x_ref, s5re0, s5im0, lruh0, conv0,
                nmix, win, bmat, abre, abim, cre, cim, dskip, wglu, bglu,
                convw, convb, wax, bax, lam, wpa, wpb, wout,
                x1_ref, s5re, s5im, lruh, convo,
                proj, bu, la, lb, xbuf, *, nb, tl, cw):
    rows = nb * tl
    d_s5 = dskip.shape[-1]
    d_lru = lam.shape[-1]
    d_model = x_ref.shape[-1]
    n_state = abre.shape[-1]
    tail = (CONV_W - 1) * nb

    @pl.when(pl.program_id(0) == 0)
    def _():
        s5re[...] = s5re0[...]
        s5im[...] = s5im0[...]
        lruh[...] = lruh0[...]
        xbuf[0:tail, :] = conv0[...]

    x = x_ref[...]
    proj[...] = _dot(_rms(x, nmix[...]), win[...])
    c0, c1, c2, c3 = d_s5, d_s5 + d_lru, d_s5 + 2 * d_lru, d_s5 + 2 * d_lru + d_model

    u = proj[:, 0:c0]
    bu[...] = _dot(u, bmat[...])
    for c in range(n_state // cw):
        lo = c * cw
        ar = jnp.broadcast_to(abre[:, lo:lo + cw], (nb, cw))
        ai = jnp.broadcast_to(abim[:, lo:lo + cw], (nb, cw))

        def s5_step(t, carry, lo=lo, ar=ar, ai=ai):
            hr, hi = carry
            r0 = pl.multiple_of(t * nb, nb)
            br = bu[pl.ds(r0, nb), lo:lo + cw]
            bi = bu[pl.ds(r0, nb), n_state + lo:n_state + lo + cw]
            nr = ar * hr - ai * hi + br
            ni = ar * hi + ai * hr + bi
            bu[pl.ds(r0, nb), lo:lo + cw] = nr
            bu[pl.ds(r0, nb), n_state + lo:n_state + lo + cw] = ni
            return nr, ni

        hr, hi = lax.fori_loop(0, tl, s5_step, (s5re[:, lo:lo + cw], s5im[:, lo:lo + cw]),
                               unroll=min(tl, 8))
        s5re[:, lo:lo + cw] = hr
        s5im[:, lo:lo + cw] = hi

    y = _dot(bu[:, 0:n_state], cre[...]) - _dot(bu[:, n_state:2 * n_state], cim[...])
    y = y + dskip[...] * u
    z = _gelu(y)
    out_a = z * _sigmoid(_dot(z, wglu[...]) + bglu[...])

    xbuf[tail:tail + rows, :] = proj[:, c0:c1]
    acc = xbuf[0:rows, :] * convw[0:1, :]
    for k in range(1, CONV_W):
        acc = acc + xbuf[k * nb:k * nb + rows, :] * convw[k:k + 1, :]
    xc = convb[...] + acc
    new_tail = xbuf[tl * nb:tl * nb + tail, :]
    convo[...] = new_tail
    xbuf[0:tail, :] = new_tail

    gates = _dot(xc, wax[...]) + bax[...]
    r_gate = _sigmoid(gates[:, 0:d_lru])
    i_gate = _sigmoid(gates[:, d_lru:2 * d_lru])
    neg_lam = -lam[...]
    softplus = jnp.maximum(neg_lam, 0.0) + jnp.log1p(jnp.exp(-jnp.abs(neg_lam)))
    log_a = (-LRU_C * r_gate) * softplus
    a_t = jnp.exp(log_a)
    la[...] = a_t
    lb[...] = jnp.sqrt(-jnp.tanh(log_a) * (a_t * a_t + 1.0)) * (i_gate * xc)

    def lru_step(t, h):
        r0 = pl.multiple_of(t * nb, nb)
        hn = la[pl.ds(r0, nb), :] * h + lb[pl.ds(r0, nb), :]
        lb[pl.ds(r0, nb), :] = hn
        return hn

    lruh[...] = lax.fori_loop(0, tl, lru_step, lruh[...], unroll=min(tl, 8))
    out_b = lb[...] * _gelu(proj[:, c1:c2])

    merged = (_sigmoid(proj[:, c2:c3]) * _dot(out_a, wpa[...])
              + _sigmoid(proj[:, c3:c3 + d_model]) * _dot(out_b, wpb[...]))
    x1_ref[...] = x + _dot(merged, wout[...])


def _mixer(x_tm, s5re0, s5im0, lruh0, conv0, w, *, nb, tl, cw):
    rows_total, d_model = x_tm.shape
    rows = nb * tl
    n_state = s5re0.shape[-1]
    d_lru = lruh0.shape[-1]
    d_in = w["win"].shape[-1]
    tail = (CONV_W - 1) * nb
    weights = (w["nmix"], w["win"], w["bmat"], w["abre"], w["abim"], w["cre"], w["cim"],
               w["dskip"], w["wglu"], w["bglu"], w["convw"], w["convb"], w["wax"], w["bax"],
               w["lam"], w["wpa"], w["wpb"], w["wout"])
    states = (s5re0, s5im0, lruh0, conv0)
    row_spec = pl.BlockSpec((rows, d_model), lambda i: (i, 0))
    return pl.pallas_call(
        functools.partial(_mixer_body, nb=nb, tl=tl, cw=cw),
        grid=(rows_total // rows,),
        in_specs=[row_spec] + [_const_spec(a.shape) for a in states + weights],
        out_specs=[row_spec] + [_const_spec(a.shape) for a in states],
        out_shape=[jax.ShapeDtypeStruct(x_tm.shape, F32)]
        + [jax.ShapeDtypeStruct(a.shape, F32) for a in states],
        scratch_shapes=[
            pltpu.VMEM((rows, d_in), F32),
            pltpu.VMEM((rows, 2 * n_state), F32),
            pltpu.VMEM((rows, d_lru), F32),
            pltpu.VMEM((rows, d_lru), F32),
            pltpu.VMEM((rows + tail, d_lru), F32),
        ],
        compiler_params=pltpu.CompilerParams(
            dimension_semantics=("arbitrary",), vmem_limit_bytes=VMEM_LIMIT),
        name="mixer",
    )(x_tm, *states, *weights)


def _peer_select_body(x1_ref, nffn, wq, keys, xn_ref, e1_ref, n1_ref, e2_ref, r2_ref,
                      vals, sc, *, tt, heads, n_keys):
    neg_inf = -jnp.inf
    xn = _rms(x1_ref[...], nffn[...]).astype(BF16)
    xn_ref[...] = xn
    q = jnp.dot(xn, wq[...], preferred_element_type=F32).astype(BF16)
    dk = keys.shape[-1]
    rowv = lax.broadcasted_iota(jnp.int32, (VAL_ROWS, LANES), 0)

    def extract(s_init):
        def rnd(k, carry):
            s, out = carry
            m = jnp.max(s, axis=0, keepdims=True)
            return jnp.where(s >= m, neg_inf, s), jnp.where(rowv == k, m, out)
        _, out = lax.fori_loop(0, N_EXTRACT, rnd, (s_init, jnp.full((VAL_ROWS, LANES), neg_inf, F32)))
        return out

    for g in range(2 * heads):
        s_t = lax.dot_general(keys[g], q[:, g * dk:(g + 1) * dk], (((1,), (1,)), ((), ())),
                              preferred_element_type=F32)
        sc[g] = s_t
        for c in range(tt // LANES):
            cs = slice(c * LANES, (c + 1) * LANES)
            vals[g, :, cs] = extract(sc[g, :, cs])

    row8 = lax.broadcasted_iota(jnp.int32, (SUBLANES, LANES), 0)
    for h in range(heads):
        for c in range(tt // LANES):
            cs = slice(c * LANES, (c + 1) * LANES)
            a1 = vals[2 * h, :, cs]
            a2 = vals[2 * h + 1, :, cs]
            cand = [a1[0:1, :] + a2]
            for p in range(1, N_EXTRACT):
                nq = N_EXTRACT // (p + 1)
                cand.append(jnp.where(row8 < nq, a1[p:p + 1, :] + a2[0:SUBLANES, :], neg_inf))
            cand = jnp.concatenate(cand, axis=0)
            cv = extract(cand)
            top = cv[0:TOPK, :]
            v1 = top[0:1, :]
            zsum = jnp.sum(jnp.exp(top - v1), axis=0, keepdims=True)
            thr = 0.5 * (cv[TOPK - 1:TOPK, :] + cv[TOPK:TOPK + 1, :])
            hs = slice(h * n_keys, (h + 1) * n_keys)
            s1 = sc[2 * h, :, cs]
            s2 = sc[2 * h + 1, :, cs]
            cnt = jnp.zeros_like(s1)
            rank = jnp.zeros_like(s2)
            for k in range(TOPK):
                cnt = cnt + jnp.where(s1 + a2[k:k + 1, :] >= thr, 1.0, 0.0)
                rank = rank + jnp.where(a2[k:k + 1, :] > s2, 1.0, 0.0)
            e1_ref[hs, cs] = jnp.exp(s1 - a1[0:1, :]) / zsum
            n1_ref[hs, cs] = cnt
            hp = slice(h * n_keys // 2, (h + 1) * n_keys // 2)
            e2_ref[hp, cs] = pltpu.bitcast(jnp.exp(s2 - a2[0:1, :]).astype(BF16), jnp.uint32)
            r2_ref[hp, cs] = pltpu.bitcast(rank.astype(BF16), jnp.uint32)


def _peer_select(x1, nffn, wq, keys, *, tt):
    t_total, d_model = x1.shape
    groups, n_keys, _ = keys.shape
    heads = groups // 2
    return pl.pallas_call(
        functools.partial(_peer_select_body, tt=tt, heads=heads, n_keys=n_keys),
        grid=(t_total // tt,),
        in_specs=[pl.BlockSpec((tt, d_model), lambda i: (i, 0)),
                  _const_spec(nffn.shape), _const_spec(wq.shape), _const_spec(keys.shape)],
        out_specs=[pl.BlockSpec((tt, d_model), lambda i: (i, 0))]
        + [pl.BlockSpec((heads * n_keys, tt), lambda i: (0, i))] * 2
        + [pl.BlockSpec((heads * n_keys // 2, tt), lambda i: (0, i))] * 2,
        out_shape=[jax.ShapeDtypeStruct((t_total, d_model), BF16)]
        + [jax.ShapeDtypeStruct((heads * n_keys, t_total), F32)] * 2
        + [jax.ShapeDtypeStruct((heads * n_keys // 2, t_total), jnp.uint32)] * 2,
        scratch_shapes=[pltpu.VMEM((groups, VAL_ROWS, tt), F32),
                        pltpu.VMEM((groups, n_keys, tt), F32)],
        compiler_params=pltpu.CompilerParams(
            dimension_semantics=("parallel",), vmem_limit_bytes=VMEM_LIMIT),
        name="peer_select",
    )(x1, nffn, wq, keys)


def _peer_dense_body(xn_ref, x1_ref, e1_ref, n1_ref, e2_ref, r2_ref, u_ref, vt_ref, nfin,
                     y_ref, acc, wgt, *, tt, ni, heads, n_keys, chunk):
    e = pl.program_id(1)

    @pl.when(e == 0)
    def _():
        acc[...] = jnp.zeros_like(acc)

    assert ni == SUBLANES
    xn = xn_ref[...]
    rows_c = chunk * n_keys

    def first_key_rows(ref, h, cs):
        return ref[pl.ds(pl.multiple_of(h * n_keys + e * ni, ni), ni), cs]

    for k in range(ni // chunk):
        r_lo = k * rows_c
        act = lax.dot_general(u_ref[r_lo:r_lo + rows_c, :], xn, (((1,), (1,)), ((), ())),
                              preferred_element_type=F32)
        for il in range(chunk):
            i_loc = k * chunk + il
            for c in range(tt // LANES):
                cs = slice(c * LANES, (c + 1) * LANES)
                bcast = lambda ref, h: jnp.broadcast_to(
                    first_key_rows(ref, h, cs)[i_loc:i_loc + 1, :], (BF16_ROWS, LANES)).astype(BF16)
                e1b = [bcast(e1_ref, h) for h in range(heads)]
                n1b = [bcast(n1_ref, h) for h in range(heads)]
                for rb in range(n_keys // BF16_ROWS):
                    r0 = rb * BF16_ROWS
                    g = jnp.zeros((BF16_ROWS, LANES), BF16)
                    for h in range(heads):
                        js = slice((h * n_keys + r0) // 2, (h * n_keys + r0 + BF16_ROWS) // 2)
                        e2 = pltpu.bitcast(e2_ref[js, cs], BF16)
                        r2 = pltpu.bitcast(r2_ref[js, cs], BF16)
                        sel = jnp.minimum(e2, jnp.maximum(n1b[h] - r2, 0.0))
                        g = g + e1b[h] * sel
                    a = act[il * n_keys + r0:il * n_keys + r0 + BF16_ROWS, cs].astype(BF16)
                    wgt[i_loc * n_keys + r0:i_loc * n_keys + r0 + BF16_ROWS, cs] = g * _gelu_lowp(a)
        acc[...] += jnp.dot(vt_ref[:, r_lo:r_lo + rows_c], wgt[r_lo:r_lo + rows_c, :],
                            preferred_element_type=F32)

    @pl.when(e == pl.num_programs(1) - 1)
    def _():
        y_ref[...] = _rms(x1_ref[...] + acc[...].T, nfin[...])


def _peer_dense(xn, x1, e1, n1, e2, r2, u_tab, vt_tab, nfin, *, tt, ni, heads):
    t_total, d_model = x1.shape
    n_keys = e1.shape[0] // heads
    te = ni * n_keys
    n_experts = u_tab.shape[0]
    tok = lambda shape: pl.BlockSpec(shape, lambda t, e: (t, 0))
    fac = pl.BlockSpec((heads * n_keys, tt), lambda t, e: (0, t))
    fac2 = pl.BlockSpec((heads * n_keys // 2, tt), lambda t, e: (0, t))
    tab = pl.BlockSpec((te, d_model), lambda t, e: (e, 0))
    tab_t = pl.BlockSpec((d_model, te), lambda t, e: (0, e))
    return pl.pallas_call(
        functools.partial(_peer_dense_body, tt=tt, ni=ni, heads=heads, n_keys=n_keys, chunk=2),
        grid=(t_total // tt, n_experts // te),
        in_specs=[tok((tt, d_model)), tok((tt, d_model)), fac, fac, fac2, fac2, tab, tab_t,
                  pl.BlockSpec(nfin.shape, lambda t, e: (0, 0))],
        out_specs=tok((tt, d_model)),
        out_shape=jax.ShapeDtypeStruct((t_total, d_model), F32),
        scratch_shapes=[pltpu.VMEM((d_model, tt), F32),
                        pltpu.VMEM((te, tt), BF16)],
        compiler_params=pltpu.CompilerParams(
            dimension_semantics=("parallel", "arbitrary"), vmem_limit_bytes=VMEM_LIMIT),
        name="peer_dense",
    )(xn, x1, e1, n1, e2, r2, u_tab, vt_tab, nfin)


def _layer(x_tm, s5re0, s5im0, lruh0, conv0, w, *, nb, tl, cw, tt, ni):
    x1, s5re, s5im, lruh, convo = _mixer(x_tm, s5re0, s5im0, lruh0, conv0, w, nb=nb, tl=tl, cw=cw)
    xn, e1, n1, e2, r2 = _peer_select(x1, w["nffn"], w["wq"], w["keys"], tt=tt)
    y = _peer_dense(xn, x1, e1, n1, e2, r2, w["u"], w["vt"], w["nfin"], tt=tt, ni=ni,
                    heads=w["keys"].shape[0] // 2)
    return y, s5re, s5im, lruh, convo


def kernel(x_prompt, x_sample, state_s5_re, state_s5_im, state_lru_h, state_conv, w_in, s5_lam_re, s5_lam_im, s5_log_dt, s5_b_re, s5_b_im, s5_c_re, s5_c_im, s5_d, s5_w_glu, s5_b_glu, conv_w, conv_b, lru_w_a, lru_b_a, lru_w_x, lru_b_x, lru_lam, w_proj_a, w_proj_b, w_out, norm_mix, norm_ffn, peer_w_q, peer_keys, peer_u, peer_v, norm_final):
    depth = w_in.shape[0]
    assert depth == 1, "single trunk layer"
    bp, seq, d_model = x_prompt.shape
    bs, dec_seq, _ = x_sample.shape
    assert dec_seq == 1
    groups, n_state_g = s5_lam_re.shape[1:]
    g_ch = s5_b_re.shape[-1]
    d_lru = lru_lam.shape[1] * lru_lam.shape[2]
    heads, _, n_keys, dk = peer_keys.shape[1:]
    n_state = groups * n_state_g
    row = lambda a: a.reshape(1, -1).astype(F32)

    abar_re, abar_im, bb_re, bb_im = _s5_discretise(
        s5_lam_re[0], s5_lam_im[0], s5_log_dt[0], s5_b_re[0], s5_b_im[0])
    bb = lambda a: _block_diag(a.reshape(groups, g_ch, n_state_g))
    cmat = lambda c: _block_diag(jnp.transpose(c, (0, 2, 1)))
    w = dict(
        nmix=row(norm_mix[0]), win=w_in[0].astype(BF16),
        bmat=jnp.concatenate([bb(bb_re), bb(bb_im)], axis=1).astype(BF16),
        abre=abar_re[::g_ch].reshape(1, n_state), abim=abar_im[::g_ch].reshape(1, n_state),
        cre=cmat(s5_c_re[0]).astype(BF16), cim=cmat(s5_c_im[0]).astype(BF16),
        dskip=row(s5_d[0]), wglu=s5_w_glu[0].astype(BF16), bglu=row(s5_b_glu[0]),
        convw=conv_w[0], convb=row(conv_b[0]),
        wax=jnp.concatenate([_block_diag(lru_w_a[0]), _block_diag(lru_w_x[0])], axis=1).astype(BF16),
        bax=jnp.concatenate([row(lru_b_a[0]), row(lru_b_x[0])], axis=1),
        lam=row(lru_lam[0]), wpa=w_proj_a[0].astype(BF16), wpb=w_proj_b[0].astype(BF16),
        wout=w_out[0].astype(BF16), nffn=row(norm_ffn[0]), wq=peer_w_q[0].astype(BF16),
        keys=peer_keys[0].reshape(2 * heads, n_keys, dk).astype(BF16),
        u=peer_u[0].astype(BF16), vt=peer_v[0].T.astype(BF16), nfin=row(norm_final),
    )

    xp_tm = jnp.transpose(x_prompt, (1, 0, 2)).reshape(seq * bp, d_model)
    zeros = lambda *s: jnp.zeros(s, F32)
    yp, p_re, p_im, p_h, p_conv = _layer(
        xp_tm, zeros(bp, n_state), zeros(bp, n_state), zeros(bp, d_lru),
        zeros((CONV_W - 1) * bp, d_lru), w, nb=bp, tl=32, cw=4 * LANES, tt=512, ni=SUBLANES)
    y_prompt = jnp.transpose(yp.reshape(seq, bp, d_model), (1, 0, 2))

    conv_tm = jnp.transpose(state_conv[0], (1, 0, 2)).reshape((CONV_W - 1) * bs, d_lru)
    ys, s_re, s_im, s_h, s_conv = _layer(
        x_sample.reshape(bs, d_model), state_s5_re[0].reshape(bs, n_state),
        state_s5_im[0].reshape(bs, n_state), state_lru_h[0], conv_tm, w,
        nb=bs, tl=1, cw=LANES, tt=LANES, ni=SUBLANES)
    y_sample = ys.reshape(bs, 1, d_model)

    st = lambda a, b: a.reshape(1, b, groups, n_state_g)
    cv = lambda a, b: jnp.transpose(a.reshape(CONV_W - 1, b, d_lru), (1, 0, 2))[None]
    return (y_prompt, y_sample,
            st(p_re, bp), st(p_im, bp), p_h[None], cv(p_conv, bp),
            st(s_re, bs), st(s_im, bs), s_h[None], cv(s_conv, bs))
```

```python
import functools
import math

import jax
import jax.numpy as jnp
from jax import lax
from jax.experimental import pallas as pl
from jax.experimental.pallas import tpu as pltpu

F32 = jnp.float32
BF16 = jnp.bfloat16

EPS = 1e-6
LRU_C = 8.0
CONV_W = 4
LANES = 128
SUBLANES = 8
BF16_ROWS = 16
TOPK = 16
VMEM_LIMIT = 56 * 1024 * 1024


def _gelu(x):
    c = math.sqrt(2.0 / math.pi)
    return x * (0.5 * (1.0 + jnp.tanh(c * (x + 0.044715 * (x * x * x)))))


def _gelu_lowp(x):
    c = math.sqrt(2.0 / math.pi)
    inner = x * (c + (c * 0.044715) * (x * x))
    return x.astype(BF16) * (0.5 + 0.5 * jnp.tanh(inner.astype(BF16)))


def _sigmoid(x):
    return 1.0 / (1.0 + jnp.exp(-x))


def _rms(x, g):
    return x * lax.rsqrt(jnp.mean(x * x, axis=-1, keepdims=True) + EPS) * g


def _dot(a, b):
    return jnp.dot(a.astype(BF16), b, preferred_element_type=F32)


def _const_spec(shape):
    nd = len(shape)
    return pl.BlockSpec(shape, lambda *_: (0,) * nd, pipeline_mode=pl.Buffered(1))


def _s5_disc_body(lr_ref, li_ref, ldt_ref, br_ref, bi_ref,
                  are_ref, aim_ref, bbr_ref, bbi_ref):
    lr = lr_ref[...]
    li = li_ref[...]
    dt = jnp.exp(ldt_ref[...])
    mag = jnp.exp(lr * dt)
    a_re = mag * jnp.cos(li * dt)
    a_im = mag * jnp.sin(li * dt)
    nr = a_re - 1.0
    ni = a_im
    den = lr * lr + li * li
    f_re = (nr * lr + ni * li) / den
    f_im = (ni * lr - nr * li) / den
    br = br_ref[...]
    bi = bi_ref[...]
    are_ref[...] = a_re
    aim_ref[...] = a_im
    bbr_ref[...] = f_re * br - f_im * bi
    bbi_ref[...] = f_re * bi + f_im * br


def _s5_discretise(lam_re, lam_im, log_dt, b_re, b_im):
    g, p, h = b_re.shape
    rep = lambda a: jnp.repeat(a, h, axis=0)
    tr = lambda b: jnp.transpose(b, (0, 2, 1)).reshape(g * h, p)
    shp = jax.ShapeDtypeStruct((g * h, p), F32)
    return pl.pallas_call(
        _s5_disc_body,
        out_shape=(shp, shp, shp, shp),
        name="s5_discretise",
    )(rep(lam_re), rep(lam_im), rep(log_dt[:, None]), tr(b_re), tr(b_im))


def _block_diag(blocks):
    g, r, c = blocks.shape
    eye = jnp.eye(g, dtype=blocks.dtype)
    return (blocks[:, :, None, :] * eye[:, None, :, None]).reshape(g * r, g * c)


def _mixer_body(x_ref, s5re0, s5im0, lruh0, conv0,
                nmix, win, bmat, abre, abim, cre, cim, dskip, wglu, bglu,
                convw, convb, wax, bax, lam, wpa, wpb, wout,
                x1_ref, s5re, s5im, lruh, convo,
                proj, bu, la, lb, xbuf, *, nb, tl, cw):
    rows = nb * tl
    d_s5 = dskip.shape[-1]
    d_lru = lam.shape[-1]
    d_model = x_ref.shape[-1]
    n_state = abre.shape[-1]
    tail = (CONV_W - 1) * nb

    @pl.when(pl.program_id(0) == 0)
    def _():
        s5re[...] = s5re0[...]
        s5im[...] = s5im0[...]
        lruh[...] = lruh0[...]
        xbuf[0:tail, :] = conv0[...]

    x = x_ref[...]
    proj[...] = _dot(_rms(x, nmix[...]), win[...])
    c0, c1, c2, c3 = d_s5, d_s5 + d_lru, d_s5 + 2 * d_lru, d_s5 + 2 * d_lru + d_model

    u = proj[:, 0:c0]
    bu[...] = _dot(u, bmat[...])
    for c in range(n_state // cw):
        lo = c * cw
        ar = jnp.broadcast_to(abre[:, lo:lo + cw], (nb, cw))
        ai = jnp.broadcast_to(abim[:, lo:lo + cw], (nb, cw))

        def s5_step(t, carry, lo=lo, ar=ar, ai=ai):
            hr, hi = carry
            r0 = pl.multiple_of(t * nb, nb)
            br = bu[pl.ds(r0, nb), lo:lo + cw]
            bi = bu[pl.ds(r0, nb), n_state + lo:n_state + lo + cw]
            nr = ar * hr - ai * hi + br
            ni = ar * hi + ai * hr + bi
            bu[pl.ds(r0, nb), lo:lo + cw] = nr
            bu[pl.ds(r0, nb), n_state + lo:n_state + lo + cw] = ni
            return nr, ni

        hr, hi = lax.fori_loop(0, tl, s5_step, (s5re[:, lo:lo + cw], s5im[:, lo:lo + cw]),
                               unroll=min(tl, 8))
        s5re[:, lo:lo + cw] = hr
        s5im[:, lo:lo + cw] = hi

    y = _dot(bu[:, 0:n_state], cre[...]) - _dot(bu[:, n_state:2 * n_state], cim[...])
    y = y + dskip[...] * u
    z = _gelu(y)
    out_a = z * _sigmoid(_dot(z, wglu[...]) + bglu[...])

    xbuf[tail:tail + rows, :] = proj[:, c0:c1]
    acc = xbuf[0:rows, :] * convw[0:1, :]
    for k in range(1, CONV_W):
        acc = acc + xbuf[k * nb:k * nb + rows, :] * convw[k:k + 1, :]
    xc = convb[...] + acc
    new_tail = xbuf[tl * nb:tl * nb + tail, :]
    convo[...] = new_tail
    xbuf[0:tail, :] = new_tail

    gates = _dot(xc, wax[...]) + bax[...]
    r_gate = _sigmoid(gates[:, 0:d_lru])
    i_gate = _sigmoid(gates[:, d_lru:2 * d_lru])
    neg_lam = -lam[...]
    softplus = jnp.maximum(neg_lam, 0.0) + jnp.log1p(jnp.exp(-jnp.abs(neg_lam)))
    log_a = (-LRU_C * r_gate) * softplus
    a_t = jnp.exp(log_a)
    la[...] = a_t
    lb[...] = jnp.sqrt(-jnp.tanh(log_a) * (a_t * a_t + 1.0)) * (i_gate * xc)

    def lru_step(t, h):
        r0 = pl.multiple_of(t * nb, nb)
        hn = la[pl.ds(r0, nb), :] * h + lb[pl.ds(r0, nb), :]
        lb[pl.ds(r0, nb), :] = hn
        return hn

    lruh[...] = lax.fori_loop(0, tl, lru_step, lruh[...], unroll=min(tl, 8))
    out_b = lb[...] * _gelu(proj[:, c1:c2])

    merged = (_sigmoid(proj[:, c2:c3]) * _dot(out_a, wpa[...])
              + _sigmoid(proj[:, c3:c3 + d_model]) * _dot(out_b, wpb[...]))
    x1_ref[...] = x + _dot(merged, wout[...])


def _mixer(x_tm, s5re0, s5im0, lruh0, conv0, w, *, nb, tl, cw):
    rows_total, d_model = x_tm.shape
    rows = nb * tl
    n_state = s5re0.shape[-1]
    d_lru = lruh0.shape[-1]
    d_in = w["win"].shape[-1]
    tail = (CONV_W - 1) * nb
    weights = (w["nmix"], w["win"], w["bmat"], w["abre"], w["abim"], w["cre"], w["cim"],
               w["dskip"], w["wglu"], w["bglu"], w["convw"], w["convb"], w["wax"], w["bax"],
               w["lam"], w["wpa"], w["wpb"], w["wout"])
    states = (s5re0, s5im0, lruh0, conv0)
    row_spec = pl.BlockSpec((rows, d_model), lambda i: (i, 0))
    return pl.pallas_call(
        functools.partial(_mixer_body, nb=nb, tl=tl, cw=cw),
        grid=(rows_total // rows,),
        in_specs=[row_spec] + [_const_spec(a.shape) for a in states + weights],
        out_specs=[row_spec] + [_const_spec(a.shape) for a in states],
        out_shape=[jax.ShapeDtypeStruct(x_tm.shape, F32)]
        + [jax.ShapeDtypeStruct(a.shape, F32) for a in states],
        scratch_shapes=[
            pltpu.VMEM((rows, d_in), F32),
            pltpu.VMEM((rows, 2 * n_state), F32),
            pltpu.VMEM((rows, d_lru), F32),
            pltpu.VMEM((rows, d_lru), F32),
            pltpu.VMEM((rows + tail, d_lru), F32),
        ],
        compiler_params=pltpu.CompilerParams(
            dimension_semantics=("arbitrary",), vmem_limit_bytes=VMEM_LIMIT),
        name="mixer",
    )(x_tm, *states, *weights)


def _peer_select_body(x1_ref, nffn, wq, keys, xn_ref, e1_ref, n1_ref, e2_ref, r2_ref,
                      vals, sc, *, tt, heads, n_keys):
    neg_inf = -jnp.inf
    xn = _rms(x1_ref[...], nffn[...]).astype(BF16)
    xn_ref[...] = xn
    q = jnp.dot(xn, wq[...], preferred_element_type=F32).astype(BF16)
    dk = keys.shape[-1]
    rowv = lax.broadcasted_iota(jnp.int32, (TOPK, LANES), 0)
    none = jnp.full((TOPK, LANES), neg_inf, F32)

    def take_max(k, s, out):
        m = jnp.max(s, axis=0, keepdims=True)
        hit = s >= m
        return jnp.where(hit, neg_inf, s), jnp.where(rowv == k, m, out), hit

    def extract_pair(sa, sb, with_rank):
        def rnd(k, carry):
            sa, sb, oa, ob, rb = carry
            sa, oa, _ = take_max(k, sa, oa)
            sb, ob, hit = take_max(k, sb, ob)
            return sa, sb, oa, ob, (jnp.where(hit, lax.convert_element_type(k, F32), rb) if with_rank else rb)
        rank0 = jnp.full(sb.shape if with_rank else (SUBLANES, LANES), float(TOPK), F32)
        _, _, oa, ob, rb = lax.fori_loop(0, TOPK, rnd, (sa, sb, none, none, rank0))
        return oa, ob, rb

    for g in range(2 * heads):
        sc[g] = lax.dot_general(keys[g], q[:, g * dk:(g + 1) * dk], (((1,), (1,)), ((), ())),
                                preferred_element_type=F32)

    row8 = lax.broadcasted_iota(jnp.int32, (SUBLANES, LANES), 0)
    cols = [slice(c * LANES, (c + 1) * LANES) for c in range(tt // LANES)]
    for h in range(heads):
        hs = slice(h * n_keys, (h + 1) * n_keys)
        hp = slice(h * n_keys // 2, (h + 1) * n_keys // 2)
        for cs in cols:
            a1, a2, rank = extract_pair(sc[2 * h, :, cs], sc[2 * h + 1, :, cs], True)
            vals[2 * h, :, cs] = a1
            vals[2 * h + 1, :, cs] = a2
            r2_ref[hp, cs] = pltpu.bitcast(rank.astype(BF16), jnp.uint32)

        def candidates(cs):
            a1 = vals[2 * h, :, cs]
            a2 = vals[2 * h + 1, :, cs]
            cand = [a1[0:1, :] + a2]
            for p in range(1, TOPK):
                nq = TOPK // (p + 1)
                cand.append(jnp.where(row8 < nq, a1[p:p + 1, :] + a2[0:SUBLANES, :], neg_inf))
            return jnp.concatenate(cand, axis=0)

        for ca, cb in zip(cols[0::2], cols[1::2]) if len(cols) > 1 else [(cols[0], cols[0])]:
            tops = extract_pair(candidates(ca), candidates(cb), False)[:2]
            for cs, top in zip((ca, cb), tops):
                a1 = vals[2 * h, :, cs]
                a2 = vals[2 * h + 1, :, cs]
                zsum = jnp.sum(jnp.exp(top - top[0:1, :]), axis=0, keepdims=True)
                thr = top[TOPK - 1:TOPK, :]
                s1 = sc[2 * h, :, cs]
                cnt = jnp.zeros_like(s1)
                for k in range(TOPK):
                    cnt = cnt + jnp.where(s1 + a2[k:k + 1, :] >= thr, 1.0, 0.0)
                e1_ref[hs, cs] = jnp.exp(s1 - a1[0:1, :]) / zsum
                n1_ref[hs, cs] = cnt
                e2_ref[hp, cs] = pltpu.bitcast(
                    jnp.exp(sc[2 * h + 1, :, cs] - a2[0:1, :]).astype(BF16), jnp.uint32)


def _peer_select(x1, nffn, wq, keys, *, tt):
    t_total, d_model = x1.shape
    groups, n_keys, _ = keys.shape
    heads = groups // 2
    fac = pl.BlockSpec((heads * n_keys, tt), lambda i: (0, i))
    fac2 = pl.BlockSpec((heads * n_keys // 2, tt), lambda i: (0, i))
    return pl.pallas_call(
        functools.partial(_peer_select_body, tt=tt, heads=heads, n_keys=n_keys),
        grid=(t_total // tt,),
        in_specs=[pl.BlockSpec((tt, d_model), lambda i: (i, 0)),
                  _const_spec(nffn.shape), _const_spec(wq.shape), _const_spec(keys.shape)],
        out_specs=[pl.BlockSpec((tt, d_model), lambda i: (i, 0)), fac, fac, fac2, fac2],
        out_shape=[jax.ShapeDtypeStruct((t_total, d_model), BF16)]
        + [jax.ShapeDtypeStruct((heads * n_keys, t_total), F32)] * 2
        + [jax.ShapeDtypeStruct((heads * n_keys // 2, t_total), jnp.uint32)] * 2,
        scratch_shapes=[pltpu.VMEM((groups, TOPK, tt), F32),
                        pltpu.VMEM((groups, n_keys, tt), F32)],
        compiler_params=pltpu.CompilerParams(
            dimension_semantics=("parallel",), vmem_limit_bytes=VMEM_LIMIT),
        name="peer_select",
    )(x1, nffn, wq, keys)


def _peer_dense_body(xn_ref, x1_ref, e1_ref, n1_ref, e2_ref, r2_ref, u_ref, vt_ref, nfin,
                     y_ref, acc, wgt, *, tt, ni, heads, n_keys, chunk):
    e = pl.program_id(1)

    @pl.when(e == 0)
    def _():
        acc[...] = jnp.zeros_like(acc)

    assert ni == SUBLANES
    xn = xn_ref[...]
    rows_c = chunk * n_keys

    def first_key_rows(ref, h, cs):
        return ref[pl.ds(pl.multiple_of(h * n_keys + e * ni, ni), ni), cs]

    for k in range(ni // chunk):
        r_lo = k * rows_c
        act = lax.dot_general(u_ref[r_lo:r_lo + rows_c, :], xn, (((1,), (1,)), ((), ())),
                              preferred_element_type=F32)
        for il in range(chunk):
            i_loc = k * chunk + il
            for c in range(tt // LANES):
                cs = slice(c * LANES, (c + 1) * LANES)
                bcast = lambda ref, h: jnp.broadcast_to(
                    first_key_rows(ref, h, cs)[i_loc:i_loc + 1, :], (BF16_ROWS, LANES)).astype(BF16)
                e1b = [bcast(e1_ref, h) for h in range(heads)]
                n1b = [bcast(n1_ref, h) for h in range(heads)]
                for rb in range(n_keys // BF16_ROWS):
                    r0 = rb * BF16_ROWS
                    gate = jnp.zeros((BF16_ROWS, LANES), BF16)
                    for h in range(heads):
                        js = slice((h * n_keys + r0) // 2, (h * n_keys + r0 + BF16_ROWS) // 2)
                        e2 = pltpu.bitcast(e2_ref[js, cs], BF16)
                        r2 = pltpu.bitcast(r2_ref[js, cs], BF16)
                        gate = gate + e1b[h] * jnp.minimum(e2, jnp.maximum(n1b[h] - r2, 0.0))
                    a = act[il * n_keys + r0:il * n_keys + r0 + BF16_ROWS, cs]
                    wgt[i_loc * n_keys + r0:i_loc * n_keys + r0 + BF16_ROWS, cs] = (
                        gate * _gelu_lowp(a))
        acc[...] += jnp.dot(vt_ref[:, r_lo:r_lo + rows_c], wgt[r_lo:r_lo + rows_c, :],
                            preferred_element_type=F32)

    @pl.when(e == pl.num_programs(1) - 1)
    def _():
        y_ref[...] = _rms(x1_ref[...] + acc[...].T, nfin[...])


def _peer_dense(xn, x1, e1, n1, e2, r2, u_tab, vt_tab, nfin, *, tt, ni, heads):
    t_total, d_model = x1.shape
    n_keys = e1.shape[0] // heads
    te = ni * n_keys
    n_experts = u_tab.shape[0]
    tok = lambda shape: pl.BlockSpec(shape, lambda t, e: (t, 0))
    fac = pl.BlockSpec((heads * n_keys, tt), lambda t, e: (0, t))
    fac2 = pl.BlockSpec((heads * n_keys // 2, tt), lambda t, e: (0, t))
    tab = pl.BlockSpec((te, d_model), lambda t, e: (e, 0))
    tab_t = pl.BlockSpec((d_model, te), lambda t, e: (0, e))
    return pl.pallas_call(
        functools.partial(_peer_dense_body, tt=tt, ni=ni, heads=heads, n_keys=n_keys, chunk=8),
        grid=(t_total // tt, n_experts // te),
        in_specs=[tok((tt, d_model)), tok((tt, d_model)), fac, fac, fac2, fac2, tab, tab_t,
                  pl.BlockSpec(nfin.shape, lambda t, e: (0, 0))],
        out_specs=tok((tt, d_model)),
        out_shape=jax.ShapeDtypeStruct((t_total, d_model), F32),
        scratch_shapes=[pltpu.VMEM((d_model, tt), F32),
                        pltpu.VMEM((te, tt), BF16)],
        compiler_params=pltpu.CompilerParams(
            dimension_semantics=("parallel", "arbitrary"), vmem_limit_bytes=VMEM_LIMIT),
        name="peer_dense",
    )(xn, x1, e1, n1, e2, r2, u_tab, vt_tab, nfin)


def _layer(x_tm, s5re0, s5im0, lruh0, conv0, w, *, nb, tl, cw, tt, ni):
    x1, s5re, s5im, lruh, convo = _mixer(x_tm, s5re0, s5im0, lruh0, conv0, w, nb=nb, tl=tl, cw=cw)
    xn, e1, n1, e2, r2 = _peer_select(x1, w["nffn"], w["wq"], w["keys"], tt=tt)
    y = _peer_dense(xn, x1, e1, n1, e2, r2, w["u"], w["vt"], w["nfin"], tt=tt, ni=ni,
                    heads=w["keys"].shape[0] // 2)
    return y, s5re, s5im, lruh, convo


def kernel(x_prompt, x_sample, state_s5_re, state_s5_im, state_lru_h, state_conv, w_in, s5_lam_re, s5_lam_im, s5_log_dt, s5_b_re, s5_b_im, s5_c_re, s5_c_im, s5_d, s5_w_glu, s5_b_glu, conv_w, conv_b, lru_w_a, lru_b_a, lru_w_x, lru_b_x, lru_lam, w_proj_a, w_proj_b, w_out, norm_mix, norm_ffn, peer_w_q, peer_keys, peer_u, peer_v, norm_final):
    depth = w_in.shape[0]
    assert depth == 1, "single trunk layer"
    bp, seq, d_model = x_prompt.shape
    bs, dec_seq, _ = x_sample.shape
    assert dec_seq == 1
    groups, n_state_g = s5_lam_re.shape[1:]
    g_ch = s5_b_re.shape[-1]
    d_lru = lru_lam.shape[1] * lru_lam.shape[2]
    heads, _, n_keys, dk = peer_keys.shape[1:]
    n_state = groups * n_state_g
    row = lambda a: a.reshape(1, -1).astype(F32)

    abar_re, abar_im, bb_re, bb_im = _s5_discretise(
        s5_lam_re[0], s5_lam_im[0], s5_log_dt[0], s5_b_re[0], s5_b_im[0])
    bb = lambda a: _block_diag(a.reshape(groups, g_ch, n_state_g))
    cmat = lambda c: _block_diag(jnp.transpose(c, (0, 2, 1)))
    w = dict(
        nmix=row(norm_mix[0]), win=w_in[0].astype(BF16),
        bmat=jnp.concatenate([bb(bb_re), bb(bb_im)], axis=1).astype(BF16),
        abre=abar_re[::g_ch].reshape(1, n_state), abim=abar_im[::g_ch].reshape(1, n_state),
        cre=cmat(s5_c_re[0]).astype(BF16), cim=cmat(s5_c_im[0]).astype(BF16),
        dskip=row(s5_d[0]), wglu=s5_w_glu[0].astype(BF16), bglu=row(s5_b_glu[0]),
        convw=conv_w[0], convb=row(conv_b[0]),
        wax=jnp.concatenate([_block_diag(lru_w_a[0]), _block_diag(lru_w_x[0])], axis=1).astype(BF16),
        bax=jnp.concatenate([row(lru_b_a[0]), row(lru_b_x[0])], axis=1),
        lam=row(lru_lam[0]), wpa=w_proj_a[0].astype(BF16), wpb=w_proj_b[0].astype(BF16),
        wout=w_out[0].astype(BF16), nffn=row(norm_ffn[0]), wq=peer_w_q[0].astype(BF16),
        keys=peer_keys[0].reshape(2 * heads, n_keys, dk).astype(BF16),
        u=peer_u[0].astype(BF16), vt=peer_v[0].T.astype(BF16), nfin=row(norm_final),
    )

    xp_tm = jnp.transpose(x_prompt, (1, 0, 2)).reshape(seq * bp, d_model)
    zeros = lambda *s: jnp.zeros(s, F32)
    yp, p_re, p_im, p_h, p_conv = _layer(
        xp_tm, zeros(bp, n_state), zeros(bp, n_state), zeros(bp, d_lru),
        zeros((CONV_W - 1) * bp, d_lru), w, nb=bp, tl=32, cw=4 * LANES, tt=512, ni=SUBLANES)
    y_prompt = jnp.transpose(yp.reshape(seq, bp, d_model), (1, 0, 2))

    conv_tm = jnp.transpose(state_conv[0], (1, 0, 2)).reshape((CONV_W - 1) * bs, d_lru)
    ys, s_re, s_im, s_h, s_conv = _layer(
        x_sample.reshape(bs, d_model), state_s5_re[0].reshape(bs, n_state),
        state_s5_im[0].reshape(bs, n_state), state_lru_h[0], conv_tm, w,
        nb=bs, tl=1, cw=LANES, tt=LANES, ni=SUBLANES)
    y_sample = ys.reshape(bs, 1, d_model)

    st = lambda a, b: a.reshape(1, b, groups, n_state_g)
    cv = lambda a, b: jnp.transpose(a.reshape(CONV_W - 1, b, d_lru), (1, 0, 2))[None]
    return (y_prompt, y_sample,
            st(p_re, bp), st(p_im, bp), p_h[None], cv(p_conv, bp),
            st(s_re, bs), st(s_im, bs), s_h[None], cv(s_conv, bs))
```

```python
import functools
import math

import jax
import jax.numpy as jnp
from jax import lax
from jax.experimental import pallas as pl
from jax.experimental.pallas import tpu as pltpu

F32 = jnp.float32
BF16 = jnp.bfloat16

EPS = 1e-6
LRU_C = 8.0
CONV_W = 4
LANES = 128
SUBLANES = 8
BF16_ROWS = 16
TOPK = 16
EXTRACT_CHAINS = 4
DIAG_SPLIT = 2
VMEM_LIMIT = 56 * 1024 * 1024


def _gelu(x):
    c = math.sqrt(2.0 / math.pi)
    return x * (0.5 * (1.0 + jnp.tanh(c * (x + 0.044715 * (x * x * x)))))


def _gelu_lowp(x):
    c = math.sqrt(2.0 / math.pi)
    inner = x * (c + (c * 0.044715) * (x * x))
    return x.astype(BF16) * (0.5 + 0.5 * jnp.tanh(inner.astype(BF16)))


def _sigmoid(x):
    return 1.0 / (1.0 + jnp.exp(-x))


def _rms(x, g):
    return x * lax.rsqrt(jnp.mean(x * x, axis=-1, keepdims=True) + EPS) * g


def _dot(a, b):
    return jnp.dot(a.astype(BF16), b, preferred_element_type=F32)


def _diag_dot(x, w_ref, col0, width, out_ref=None):
    k = x.shape[-1] // DIAG_SPLIT
    n = width // DIAG_SPLIT
    parts = []
    for b in range(DIAG_SPLIT):
        cols = slice(col0 + b * n, col0 + (b + 1) * n)
        part = _dot(x[:, b * k:(b + 1) * k], w_ref[b * k:(b + 1) * k, cols])
        if out_ref is None:
            parts.append(part)
        else:
            out_ref[:, cols] = part
    return None if out_ref is not None else jnp.concatenate(parts, axis=-1)


def _const_spec(shape):
    nd = len(shape)
    return pl.BlockSpec(shape, lambda *_: (0,) * nd, pipeline_mode=pl.Buffered(1))


def _s5_disc_body(lr_ref, li_ref, ldt_ref, br_ref, bi_ref,
                  are_ref, aim_ref, bbr_ref, bbi_ref):
    lr = lr_ref[...]
    li = li_ref[...]
    dt = jnp.exp(ldt_ref[...])
    mag = jnp.exp(lr * dt)
    a_re = mag * jnp.cos(li * dt)
    a_im = mag * jnp.sin(li * dt)
    nr = a_re - 1.0
    ni = a_im
    den = lr * lr + li * li
    f_re = (nr * lr + ni * li) / den
    f_im = (ni * lr - nr * li) / den
    br = br_ref[...]
    bi = bi_ref[...]
    are_ref[...] = a_re
    aim_ref[...] = a_im
    bbr_ref[...] = f_re * br - f_im * bi
    bbi_ref[...] = f_re * bi + f_im * br


def _s5_discretise(lam_re, lam_im, log_dt, b_re, b_im):
    g, p, h = b_re.shape
    rep = lambda a: jnp.repeat(a, h, axis=0)
    tr = lambda b: jnp.transpose(b, (0, 2, 1)).reshape(g * h, p)
    shp = jax.ShapeDtypeStruct((g * h, p), F32)
    return pl.pallas_call(
        _s5_disc_body,
        out_shape=(shp, shp, shp, shp),
        name="s5_discretise",
    )(rep(lam_re), rep(lam_im), rep(log_dt[:, None]), tr(b_re), tr(b_im))


def _block_diag(blocks):
    g, r, c = blocks.shape
    eye = jnp.eye(g, dtype=blocks.dtype)
    return (blocks[:, :, None, :] * eye[:, None, :, None]).reshape(g * r, g * c)


def _mixer_body(x_ref, s5re0, s5im0, lruh0, conv0,
                nmix, win, bmat, abre, abim, cre, cim, dskip, wglu, bglu,
                convw, convb, wax, bax, lam, wpa, wpb, wout,
                x1_ref, s5re, s5im, lruh, convo,
                proj, bu, la, lb, xbuf, *, nb, tl, cw):
    rows = nb * tl
    d_s5 = dskip.shape[-1]
    d_lru = lam.shape[-1]
    d_model = x_ref.shape[-1]
    n_state = abre.shape[-1]
    tail = (CONV_W - 1) * nb

    @pl.when(pl.program_id(0) == 0)
    def _():
        s5re[...] = s5re0[...]
        s5im[...] = s5im0[...]
        lruh[...] = lruh0[...]
        xbuf[0:tail, :] = conv0[...]

    x = x_ref[...]
    proj[...] = _dot(_rms(x, nmix[...]), win[...])
    c0, c1, c2, c3 = d_s5, d_s5 + d_lru, d_s5 + 2 * d_lru, d_s5 + 2 * d_lru + d_model

    u = proj[:, 0:c0]
    _diag_dot(u, bmat, 0, n_state, out_ref=bu)
    _diag_dot(u, bmat, n_state, n_state, out_ref=bu)
    for c in range(n_state // cw):
        lo = c * cw
        ar = jnp.broadcast_to(abre[:, lo:lo + cw], (nb, cw))
        ai = jnp.broadcast_to(abim[:, lo:lo + cw], (nb, cw))

        def s5_step(t, carry, lo=lo, ar=ar, ai=ai):
            hr, hi = carry
            r0 = pl.multiple_of(t * nb, nb)
            br = bu[pl.ds(r0, nb), lo:lo + cw]
            bi = bu[pl.ds(r0, nb), n_state + lo:n_state + lo + cw]
            nr = ar * hr - ai * hi + br
            ni = ar * hi + ai * hr + bi
            bu[pl.ds(r0, nb), lo:lo + cw] = nr
            bu[pl.ds(r0, nb), n_state + lo:n_state + lo + cw] = ni
            return nr, ni

        hr, hi = lax.fori_loop(0, tl, s5_step, (s5re[:, lo:lo + cw], s5im[:, lo:lo + cw]),
                               unroll=min(tl, 8))
        s5re[:, lo:lo + cw] = hr
        s5im[:, lo:lo + cw] = hi

    y = (_diag_dot(bu[:, 0:n_state], cre, 0, d_s5)
         - _diag_dot(bu[:, n_state:2 * n_state], cim, 0, d_s5))
    y = y + dskip[...] * u
    z = _gelu(y)
    out_a = z * _sigmoid(_dot(z, wglu[...]) + bglu[...])

    xbuf[tail:tail + rows, :] = proj[:, c0:c1]
    acc = xbuf[0:rows, :] * convw[0:1, :]
    for k in range(1, CONV_W):
        acc = acc + xbuf[k * nb:k * nb + rows, :] * convw[k:k + 1, :]
    xc = convb[...] + acc
    new_tail = xbuf[tl * nb:tl * nb + tail, :]
    convo[...] = new_tail
    xbuf[0:tail, :] = new_tail

    r_gate = _sigmoid(_diag_dot(xc, wax, 0, d_lru) + bax[:, 0:d_lru])
    i_gate = _sigmoid(_diag_dot(xc, wax, d_lru, d_lru) + bax[:, d_lru:2 * d_lru])
    neg_lam = -lam[...]
    softplus = jnp.maximum(neg_lam, 0.0) + jnp.log1p(jnp.exp(-jnp.abs(neg_lam)))
    log_a = (-LRU_C * r_gate) * softplus
    a_t = jnp.exp(log_a)
    la[...] = a_t
    lb[...] = jnp.sqrt(-jnp.tanh(log_a) * (a_t * a_t + 1.0)) * (i_gate * xc)

    def lru_step(t, h):
        r0 = pl.multiple_of(t * nb, nb)
        hn = la[pl.ds(r0, nb), :] * h + lb[pl.ds(r0, nb), :]
        lb[pl.ds(r0, nb), :] = hn
        return hn

    lruh[...] = lax.fori_loop(0, tl, lru_step, lruh[...], unroll=min(tl, 8))
    out_b = lb[...] * _gelu(proj[:, c1:c2])

    merged = (_sigmoid(proj[:, c2:c3]) * _dot(out_a, wpa[...])
              + _sigmoid(proj[:, c3:c3 + d_model]) * _dot(out_b, wpb[...]))
    x1_ref[...] = x + _dot(merged, wout[...])


def _mixer(x_tm, s5re0, s5im0, lruh0, conv0, w, *, nb, tl, cw):
    rows_total, d_model = x_tm.shape
    rows = nb * tl
    n_state = s5re0.shape[-1]
    d_lru = lruh0.shape[-1]
    d_in = w["win"].shape[-1]
    tail = (CONV_W - 1) * nb
    weights = (w["nmix"], w["win"], w["bmat"], w["abre"], w["abim"], w["cre"], w["cim"],
               w["dskip"], w["wglu"], w["bglu"], w["convw"], w["convb"], w["wax"], w["bax"],
               w["lam"], w["wpa"], w["wpb"], w["wout"])
    states = (s5re0, s5im0, lruh0, conv0)
    row_spec = pl.BlockSpec((rows, d_model), lambda i: (i, 0))
    return pl.pallas_call(
        functools.partial(_mixer_body, nb=nb, tl=tl, cw=cw),
        grid=(rows_total // rows,),
        in_specs=[row_spec] + [_const_spec(a.shape) for a in states + weights],
        out_specs=[row_spec] + [_const_spec(a.shape) for a in states],
        out_shape=[jax.ShapeDtypeStruct(x_tm.shape, F32)]
        + [jax.ShapeDtypeStruct(a.shape, F32) for a in states],
        scratch_shapes=[
            pltpu.VMEM((rows, d_in), F32),
            pltpu.VMEM((rows, 2 * n_state), F32),
            pltpu.VMEM((rows, d_lru), F32),
            pltpu.VMEM((rows, d_lru), F32),
            pltpu.VMEM((rows + tail, d_lru), F32),
        ],
        compiler_params=pltpu.CompilerParams(
            dimension_semantics=("arbitrary",), vmem_limit_bytes=VMEM_LIMIT),
        name="mixer",
    )(x_tm, *states, *weights)


def _peer_select_body(x1_ref, nffn, wq, keys, xn_ref, e1_ref, n1_ref, e2_ref, r2_ref,
                      vals, sc, cand, *, tt, heads, n_keys):
    neg_inf = -jnp.inf
    xn = _rms(x1_ref[...], nffn[...]).astype(BF16)
    xn_ref[...] = xn
    q = jnp.dot(xn, wq[...], preferred_element_type=F32).astype(BF16)
    dk = keys.shape[-1]
    rowv = lax.broadcasted_iota(jnp.int32, (TOPK, LANES), 0)

    def extract(loads):
        def rnd(k, carry):
            nxt = []
            for load, (m_prev, out) in zip(loads, carry):
                s = load()
                m = jnp.max(jnp.where(s < m_prev, s, neg_inf), axis=0, keepdims=True)
                nxt.append((m, jnp.where(rowv == k, m, out)))
            return tuple(nxt)
        init = tuple((jnp.full((1, LANES), jnp.inf, F32), jnp.full((TOPK, LANES), neg_inf, F32))
                     for _ in loads)
        return [out for _, out in lax.fori_loop(0, TOPK, rnd, init)]

    def extract_all(loads):
        outs = []
        for i in range(0, len(loads), EXTRACT_CHAINS):
            outs += extract(loads[i:i + EXTRACT_CHAINS])
        return outs

    for g in range(2 * heads):
        sc[g] = lax.dot_general(keys[g], q[:, g * dk:(g + 1) * dk], (((1,), (1,)), ((), ())),
                                preferred_element_type=F32)

    row8 = lax.broadcasted_iota(jnp.int32, (SUBLANES, LANES), 0)
    cols = [slice(c * LANES, (c + 1) * LANES) for c in range(tt // LANES)]
    for h in range(heads):
        hs = slice(h * n_keys, (h + 1) * n_keys)
        hp = slice(h * n_keys // 2, (h + 1) * n_keys // 2)
        groups = [(g, cs) for g in (2 * h, 2 * h + 1) for cs in cols]
        tops = extract_all([functools.partial(lambda g, cs: sc[g, :, cs], g, cs) for g, cs in groups])
        for (g, cs), top in zip(groups, tops):
            vals[g, :, cs] = top

        for cs in cols:
            a1 = vals[2 * h, :, cs]
            a2 = vals[2 * h + 1, :, cs]
            cand[0:TOPK, cs] = a1[0:1, :] + a2
            for p in range(1, TOPK):
                nq = TOPK // (p + 1)
                cand[TOPK + (p - 1) * SUBLANES:TOPK + p * SUBLANES, cs] = jnp.where(
                    row8 < nq, a1[p:p + 1, :] + a2[0:SUBLANES, :], neg_inf)
        sums = extract_all([functools.partial(lambda cs: cand[:, cs], cs) for cs in cols])

        for cs, top in zip(cols, sums):
            a1 = vals[2 * h, :, cs]
            a2 = vals[2 * h + 1, :, cs]
            zsum = jnp.sum(jnp.exp(top - top[0:1, :]), axis=0, keepdims=True)
            thr = top[TOPK - 1:TOPK, :]
            s1 = sc[2 * h, :, cs]
            s2 = sc[2 * h + 1, :, cs]
            cnt = jnp.zeros_like(s1)
            rank = jnp.zeros_like(s2)
            for k in range(TOPK):
                cnt = cnt + jnp.where(s1 + a2[k:k + 1, :] >= thr, 1.0, 0.0)
                rank = rank + jnp.where(a2[k:k + 1, :] > s2, 1.0, 0.0)
            e1_ref[hs, cs] = jnp.exp(s1 - a1[0:1, :]) / zsum
            n1_ref[hs, cs] = cnt
            e2_ref[hp, cs] = pltpu.bitcast(jnp.exp(s2 - a2[0:1, :]).astype(BF16), jnp.uint32)
            r2_ref[hp, cs] = pltpu.bitcast(rank.astype(BF16), jnp.uint32)


def _peer_select(x1, nffn, wq, keys, *, tt):
    t_total, d_model = x1.shape
    groups, n_keys, _ = keys.shape
    heads = groups // 2
    fac = pl.BlockSpec((heads * n_keys, tt), lambda i: (0, i))
    fac2 = pl.BlockSpec((heads * n_keys // 2, tt), lambda i: (0, i))
    return pl.pallas_call(
        functools.partial(_peer_select_body, tt=tt, heads=heads, n_keys=n_keys),
        grid=(t_total // tt,),
        in_specs=[pl.BlockSpec((tt, d_model), lambda i: (i, 0)),
                  _const_spec(nffn.shape), _const_spec(wq.shape), _const_spec(keys.shape)],
        out_specs=[pl.BlockSpec((tt, d_model), lambda i: (i, 0)), fac, fac, fac2, fac2],
        out_shape=[jax.ShapeDtypeStruct((t_total, d_model), BF16)]
        + [jax.ShapeDtypeStruct((heads * n_keys, t_total), F32)] * 2
        + [jax.ShapeDtypeStruct((heads * n_keys // 2, t_total), jnp.uint32)] * 2,
        scratch_shapes=[pltpu.VMEM((groups, TOPK, tt), F32),
                        pltpu.VMEM((groups, n_keys, tt), F32),
                        pltpu.VMEM((TOPK + (TOPK - 1) * SUBLANES, tt), F32)],
        compiler_params=pltpu.CompilerParams(
            dimension_semantics=("parallel",), vmem_limit_bytes=VMEM_LIMIT),
        name="peer_select",
    )(x1, nffn, wq, keys)


def _peer_dense_body(xn_ref, x1_ref, e1_ref, n1_ref, e2_ref, r2_ref, u_ref, vt_ref, nfin,
                     y_ref, acc, wgt, *, tt, ni, heads, n_keys, chunk):
    e = pl.program_id(1)

    @pl.when(e == 0)
    def _():
        acc[...] = jnp.zeros_like(acc)

    assert ni == SUBLANES
    xn = xn_ref[...]
    rows_c = chunk * n_keys

    def first_key_rows(ref, h, cs):
        return ref[pl.ds(pl.multiple_of(h * n_keys + e * ni, ni), ni), cs]

    for k in range(ni // chunk):
        r_lo = k * rows_c
        act = lax.dot_general(u_ref[r_lo:r_lo + rows_c, :], xn, (((1,), (1,)), ((), ())),
                              preferred_element_type=F32)
        for il in range(chunk):
            i_loc = k * chunk + il
            for c in range(tt // LANES):
                cs = slice(c * LANES, (c + 1) * LANES)
                bcast = lambda ref, h: jnp.broadcast_to(
                    first_key_rows(ref, h, cs)[i_loc:i_loc + 1, :], (BF16_ROWS, LANES)).astype(BF16)
                e1b = [bcast(e1_ref, h) for h in range(heads)]
                n1b = [bcast(n1_ref, h) for h in range(heads)]
                for rb in range(n_keys // BF16_ROWS):
                    r0 = rb * BF16_ROWS
                    gate = jnp.zeros((BF16_ROWS, LANES), BF16)
                    for h in range(heads):
                        js = slice((h * n_keys + r0) // 2, (h * n_keys + r0 + BF16_ROWS) // 2)
                        e2 = pltpu.bitcast(e2_ref[js, cs], BF16)
                        r2 = pltpu.bitcast(r2_ref[js, cs], BF16)
                        gate = gate + e1b[h] * jnp.minimum(e2, jnp.maximum(n1b[h] - r2, 0.0))
                    a = act[il * n_keys + r0:il * n_keys + r0 + BF16_ROWS, cs]
                    wgt[i_loc * n_keys + r0:i_loc * n_keys + r0 + BF16_ROWS, cs] = (
                        gate * _gelu_lowp(a))
        acc[...] += jnp.dot(vt_ref[:, r_lo:r_lo + rows_c], wgt[r_lo:r_lo + rows_c, :],
                            preferred_element_type=F32)

    @pl.when(e == pl.num_programs(1) - 1)
    def _():
        y_ref[...] = _rms(x1_ref[...] + acc[...].T, nfin[...])


def _peer_dense(xn, x1, e1, n1, e2, r2, u_tab, vt_tab, nfin, *, tt, ni, heads):
    t_total, d_model = x1.shape
    n_keys = e1.shape[0] // heads
    te = ni * n_keys
    n_experts = u_tab.shape[0]
    tok = lambda shape: pl.BlockSpec(shape, lambda t, e: (t, 0))
    fac = pl.BlockSpec((heads * n_keys, tt), lambda t, e: (0, t))
    fac2 = pl.BlockSpec((heads * n_keys // 2, tt), lambda t, e: (0, t))
    tab = pl.BlockSpec((te, d_model), lambda t, e: (e, 0))
    tab_t = pl.BlockSpec((d_model, te), lambda t, e: (0, e))
    return pl.pallas_call(
        functools.partial(_peer_dense_body, tt=tt, ni=ni, heads=heads, n_keys=n_keys, chunk=8),
        grid=(t_total // tt, n_experts // te),
        in_specs=[tok((tt, d_model)), tok((tt, d_model)), fac, fac, fac2, fac2, tab, tab_t,
                  pl.BlockSpec(nfin.shape, lambda t, e: (0, 0))],
        out_specs=tok((tt, d_model)),
        out_shape=jax.ShapeDtypeStruct((t_total, d_model), F32),
        scratch_shapes=[pltpu.VMEM((d_model, tt), F32),
                        pltpu.VMEM((te, tt), BF16)],
        compiler_params=pltpu.CompilerParams(
            dimension_semantics=("parallel", "arbitrary"), vmem_limit_bytes=VMEM_LIMIT),
        name="peer_dense",
    )(xn, x1, e1, n1, e2, r2, u_tab, vt_tab, nfin)


def _layer(x_tm, s5re0, s5im0, lruh0, conv0, w, *, nb, tl, cw, tt, ni):
    x1, s5re, s5im, lruh, convo = _mixer(x_tm, s5re0, s5im0, lruh0, conv0, w, nb=nb, tl=tl, cw=cw)
    xn, e1, n1, e2, r2 = _peer_select(x1, w["nffn"], w["wq"], w["keys"], tt=tt)
    y = _peer_dense(xn, x1, e1, n1, e2, r2, w["u"], w["vt"], w["nfin"], tt=tt, ni=ni,
                    heads=w["keys"].shape[0] // 2)
    return y, s5re, s5im, lruh, convo


def kernel(x_prompt, x_sample, state_s5_re, state_s5_im, state_lru_h, state_conv, w_in, s5_lam_re, s5_lam_im, s5_log_dt, s5_b_re, s5_b_im, s5_c_re, s5_c_im, s5_d, s5_w_glu, s5_b_glu, conv_w, conv_b, lru_w_a, lru_b_a, lru_w_x, lru_b_x, lru_lam, w_proj_a, w_proj_b, w_out, norm_mix, norm_ffn, peer_w_q, peer_keys, peer_u, peer_v, norm_final):
    depth = w_in.shape[0]
    assert depth == 1, "single trunk layer"
    bp, seq, d_model = x_prompt.shape
    bs, dec_seq, _ = x_sample.shape
    assert dec_seq == 1
    groups, n_state_g = s5_lam_re.shape[1:]
    g_ch = s5_b_re.shape[-1]
    d_lru = lru_lam.shape[1] * lru_lam.shape[2]
    heads, _, n_keys, dk = peer_keys.shape[1:]
    n_state = groups * n_state_g
    row = lambda a: a.reshape(1, -1).astype(F32)

    abar_re, abar_im, bb_re, bb_im = _s5_discretise(
        s5_lam_re[0], s5_lam_im[0], s5_log_dt[0], s5_b_re[0], s5_b_im[0])
    bb = lambda a: _block_diag(a.reshape(groups, g_ch, n_state_g))
    cmat = lambda c: _block_diag(jnp.transpose(c, (0, 2, 1)))
    w = dict(
        nmix=row(norm_mix[0]), win=w_in[0].astype(BF16),
        bmat=jnp.concatenate([bb(bb_re), bb(bb_im)], axis=1).astype(BF16),
        abre=abar_re[::g_ch].reshape(1, n_state), abim=abar_im[::g_ch].reshape(1, n_state),
        cre=cmat(s5_c_re[0]).astype(BF16), cim=cmat(s5_c_im[0]).astype(BF16),
        dskip=row(s5_d[0]), wglu=s5_w_glu[0].astype(BF16), bglu=row(s5_b_glu[0]),
        convw=conv_w[0], convb=row(conv_b[0]),
        wax=jnp.concatenate([_block_diag(lru_w_a[0]), _block_diag(lru_w_x[0])], axis=1).astype(BF16),
        bax=jnp.concatenate([row(lru_b_a[0]), row(lru_b_x[0])], axis=1),
        lam=row(lru_lam[0]), wpa=w_proj_a[0].astype(BF16), wpb=w_proj_b[0].astype(BF16),
        wout=w_out[0].astype(BF16), nffn=row(norm_ffn[0]), wq=peer_w_q[0].astype(BF16),
        keys=peer_keys[0].reshape(2 * heads, n_keys, dk).astype(BF16),
        u=peer_u[0].astype(BF16), vt=peer_v[0].T.astype(BF16), nfin=row(norm_final),
    )

    xp_tm = jnp.transpose(x_prompt, (1, 0, 2)).reshape(seq * bp, d_model)
    zeros = lambda *s: jnp.zeros(s, F32)
    yp, p_re, p_im, p_h, p_conv = _layer(
        xp_tm, zeros(bp, n_state), zeros(bp, n_state), zeros(bp, d_lru),
        zeros((CONV_W - 1) * bp, d_lru), w, nb=bp, tl=64, cw=4 * LANES, tt=512, ni=SUBLANES)
    y_prompt = jnp.transpose(yp.reshape(seq, bp, d_model), (1, 0, 2))

    conv_tm = jnp.transpose(state_conv[0], (1, 0, 2)).reshape((CONV_W - 1) * bs, d_lru)
    ys, s_re, s_im, s_h, s_conv = _layer(
        x_sample.reshape(bs, d_model), state_s5_re[0].reshape(bs, n_state),
        state_s5_im[0].reshape(bs, n_state), state_lru_h[0], conv_tm, w,
        nb=bs, tl=1, cw=LANES, tt=LANES, ni=SUBLANES)
    y_sample = ys.reshape(bs, 1, d_model)

    st = lambda a, b: a.reshape(1, b, groups, n_state_g)
    cv = lambda a, b: jnp.transpose(a.reshape(CONV_W - 1, b, d_lru), (1, 0, 2))[None]
    return (y_prompt, y_sample,
            st(p_re, bp), st(p_im, bp), p_h[None], cv(p_conv, bp),
            st(s_re, bs), st(s_im, bs), s_h[None], cv(s_conv, bs))
```

```python
import functools
import math

import jax
import jax.numpy as jnp
from jax import lax
from jax.experimental import pallas as pl
from jax.experimental.pallas import tpu as pltpu

F32 = jnp.float32
BF16 = jnp.bfloat16

EPS = 1e-6
LRU_C = 8.0
CONV_W = 4
LANES = 128
SUBLANES = 8
BF16_ROWS = 16
TOPK = 16
EXTRACT_CHAINS = 4
DIAG_SPLIT = 2
VMEM_LIMIT = 56 * 1024 * 1024


def _gelu(x):
    c = math.sqrt(2.0 / math.pi)
    return x * (0.5 * (1.0 + jnp.tanh(c * (x + 0.044715 * (x * x * x)))))


def _gelu_lowp(x):
    c = math.sqrt(2.0 / math.pi)
    inner = x * (c + (c * 0.044715) * (x * x))
    return x.astype(BF16) * (0.5 + 0.5 * jnp.tanh(inner.astype(BF16)))


def _sigmoid(x):
    return 1.0 / (1.0 + jnp.exp(-x))


def _rms(x, g):
    return x * lax.rsqrt(jnp.mean(x * x, axis=-1, keepdims=True) + EPS) * g


def _dot(a, b):
    return jnp.dot(a.astype(BF16), b, preferred_element_type=F32)


def _diag_dot(x, w_ref, col0, width, out_ref=None):
    k = x.shape[-1] // DIAG_SPLIT
    n = width // DIAG_SPLIT
    parts = []
    for b in range(DIAG_SPLIT):
        cols = slice(col0 + b * n, col0 + (b + 1) * n)
        part = _dot(x[:, b * k:(b + 1) * k], w_ref[b * k:(b + 1) * k, cols])
        if out_ref is None:
            parts.append(part)
        else:
            out_ref[:, cols] = part
    return None if out_ref is not None else jnp.concatenate(parts, axis=-1)


def _const_spec(shape):
    nd = len(shape)
    return pl.BlockSpec(shape, lambda *_: (0,) * nd, pipeline_mode=pl.Buffered(1))


def _s5_disc_body(lr_ref, li_ref, ldt_ref, br_ref, bi_ref,
                  are_ref, aim_ref, bbr_ref, bbi_ref):
    lr = lr_ref[...]
    li = li_ref[...]
    dt = jnp.exp(ldt_ref[...])
    mag = jnp.exp(lr * dt)
    a_re = mag * jnp.cos(li * dt)
    a_im = mag * jnp.sin(li * dt)
    nr = a_re - 1.0
    ni = a_im
    den = lr * lr + li * li
    f_re = (nr * lr + ni * li) / den
    f_im = (ni * lr - nr * li) / den
    br = br_ref[...]
    bi = bi_ref[...]
    are_ref[...] = a_re
    aim_ref[...] = a_im
    bbr_ref[...] = f_re * br - f_im * bi
    bbi_ref[...] = f_re * bi + f_im * br


def _s5_discretise(lam_re, lam_im, log_dt, b_re, b_im):
    g, p, h = b_re.shape
    rep = lambda a: jnp.repeat(a, h, axis=0)
    tr = lambda b: jnp.transpose(b, (0, 2, 1)).reshape(g * h, p)
    shp = jax.ShapeDtypeStruct((g * h, p), F32)
    return pl.pallas_call(
        _s5_disc_body,
        out_shape=(shp, shp, shp, shp),
        name="s5_discretise",
    )(rep(lam_re), rep(lam_im), rep(log_dt[:, None]), tr(b_re), tr(b_im))


def _block_diag(blocks):
    g, r, c = blocks.shape
    eye = jnp.eye(g, dtype=blocks.dtype)
    return (blocks[:, :, None, :] * eye[:, None, :, None]).reshape(g * r, g * c)


def _mixer_body(x_ref, s5re0, s5im0, lruh0, conv0,
                nmix, win, bmat, abre, abim, cre, cim, dskip, wglu, bglu,
                convw, convb, wax, bax, lam, wpa, wpb, wout,
                x1_ref, s5re, s5im, lruh, convo,
                proj, bu, la, lb, xbuf, *, nb, tl, cw):
    rows = nb * tl
    d_s5 = dskip.shape[-1]
    d_lru = lam.shape[-1]
    d_model = x_ref.shape[-1]
    n_state = abre.shape[-1]
    tail = (CONV_W - 1) * nb

    @pl.when(pl.program_id(0) == 0)
    def _():
        s5re[...] = s5re0[...]
        s5im[...] = s5im0[...]
        lruh[...] = lruh0[...]
        xbuf[0:tail, :] = conv0[...]

    x = x_ref[...]
    proj[...] = _dot(_rms(x, nmix[...]), win[...])
    c0, c1, c2, c3 = d_s5, d_s5 + d_lru, d_s5 + 2 * d_lru, d_s5 + 2 * d_lru + d_model

    u = proj[:, 0:c0]
    _diag_dot(u, bmat, 0, n_state, out_ref=bu)
    _diag_dot(u, bmat, n_state, n_state, out_ref=bu)
    for c in range(n_state // cw):
        lo = c * cw
        ar = jnp.broadcast_to(abre[:, lo:lo + cw], (nb, cw))
        ai = jnp.broadcast_to(abim[:, lo:lo + cw], (nb, cw))

        def s5_step(t, carry, lo=lo, ar=ar, ai=ai):
            hr, hi = carry
            r0 = pl.multiple_of(t * nb, nb)
            br = bu[pl.ds(r0, nb), lo:lo + cw]
            bi = bu[pl.ds(r0, nb), n_state + lo:n_state + lo + cw]
            nr = ar * hr - ai * hi + br
            ni = ar * hi + ai * hr + bi
            bu[pl.ds(r0, nb), lo:lo + cw] = nr
            bu[pl.ds(r0, nb), n_state + lo:n_state + lo + cw] = ni
            return nr, ni

        hr, hi = lax.fori_loop(0, tl, s5_step, (s5re[:, lo:lo + cw], s5im[:, lo:lo + cw]),
                               unroll=min(tl, 8))
        s5re[:, lo:lo + cw] = hr
        s5im[:, lo:lo + cw] = hi

    y = (_diag_dot(bu[:, 0:n_state], cre, 0, d_s5)
         - _diag_dot(bu[:, n_state:2 * n_state], cim, 0, d_s5))
    y = y + dskip[...] * u
    z = _gelu(y)
    out_a = z * _sigmoid(_dot(z, wglu[...]) + bglu[...])

    xbuf[tail:tail + rows, :] = proj[:, c0:c1]
    acc = xbuf[0:rows, :] * convw[0:1, :]
    for k in range(1, CONV_W):
        acc = acc + xbuf[k * nb:k * nb + rows, :] * convw[k:k + 1, :]
    xc = convb[...] + acc
    new_tail = xbuf[tl * nb:tl * nb + tail, :]
    convo[...] = new_tail
    xbuf[0:tail, :] = new_tail

    r_gate = _sigmoid(_diag_dot(xc, wax, 0, d_lru) + bax[:, 0:d_lru])
    i_gate = _sigmoid(_diag_dot(xc, wax, d_lru, d_lru) + bax[:, d_lru:2 * d_lru])
    neg_lam = -lam[...]
    softplus = jnp.maximum(neg_lam, 0.0) + jnp.log1p(jnp.exp(-jnp.abs(neg_lam)))
    log_a = (-LRU_C * r_gate) * softplus
    a_t = jnp.exp(log_a)
    la[...] = a_t
    lb[...] = jnp.sqrt(-jnp.tanh(log_a) * (a_t * a_t + 1.0)) * (i_gate * xc)

    def lru_step(t, h):
        r0 = pl.multiple_of(t * nb, nb)
        hn = la[pl.ds(r0, nb), :] * h + lb[pl.ds(r0, nb), :]
        lb[pl.ds(r0, nb), :] = hn
        return hn

    lruh[...] = lax.fori_loop(0, tl, lru_step, lruh[...], unroll=min(tl, 8))
    out_b = lb[...] * _gelu(proj[:, c1:c2])

    merged = (_sigmoid(proj[:, c2:c3]) * _dot(out_a, wpa[...])
              + _sigmoid(proj[:, c3:c3 + d_model]) * _dot(out_b, wpb[...]))
    x1_ref[...] = x + _dot(merged, wout[...])


def _mixer(x_tm, s5re0, s5im0, lruh0, conv0, w, *, nb, tl, cw):
    rows_total, d_model = x_tm.shape
    rows = nb * tl
    n_state = s5re0.shape[-1]
    d_lru = lruh0.shape[-1]
    d_in = w["win"].shape[-1]
    tail = (CONV_W - 1) * nb
    weights = (w["nmix"], w["win"], w["bmat"], w["abre"], w["abim"], w["cre"], w["cim"],
               w["dskip"], w["wglu"], w["bglu"], w["convw"], w["convb"], w["wax"], w["bax"],
               w["lam"], w["wpa"], w["wpb"], w["wout"])
    states = (s5re0, s5im0, lruh0, conv0)
    row_spec = pl.BlockSpec((rows, d_model), lambda i: (i, 0))
    return pl.pallas_call(
        functools.partial(_mixer_body, nb=nb, tl=tl, cw=cw),
        grid=(rows_total // rows,),
        in_specs=[row_spec] + [_const_spec(a.shape) for a in states + weights],
        out_specs=[row_spec] + [_const_spec(a.shape) for a in states],
        out_shape=[jax.ShapeDtypeStruct(x_tm.shape, F32)]
        + [jax.ShapeDtypeStruct(a.shape, F32) for a in states],
        scratch_shapes=[
            pltpu.VMEM((rows, d_in), F32),
            pltpu.VMEM((rows, 2 * n_state), F32),
            pltpu.VMEM((rows, d_lru), F32),
            pltpu.VMEM((rows, d_lru), F32),
            pltpu.VMEM((rows + tail, d_lru), F32),
        ],
        compiler_params=pltpu.CompilerParams(
            dimension_semantics=("arbitrary",), vmem_limit_bytes=VMEM_LIMIT),
        name="mixer",
    )(x_tm, *states, *weights)


def _peer_select_body(x1_ref, nffn, wq, keys, xn_ref, e1_ref, n1_ref, e2_ref, r2_ref,
                      vals, sc, cand, *, tt, heads, n_keys):
    neg_inf = -jnp.inf
    xn = _rms(x1_ref[...], nffn[...]).astype(BF16)
    xn_ref[...] = xn
    q = jnp.dot(xn, wq[...], preferred_element_type=F32).astype(BF16)
    dk = keys.shape[-1]
    rowv = lax.broadcasted_iota(jnp.int32, (TOPK, LANES), 0)

    def extract(loads):
        def rnd(k, carry):
            nxt = []
            for load, (m_prev, out) in zip(loads, carry):
                s = load()
                m = jnp.max(jnp.where(s < m_prev, s, neg_inf), axis=0, keepdims=True)
                nxt.append((m, jnp.where(rowv == k, m, out)))
            return tuple(nxt)
        init = tuple((jnp.full((1, LANES), jnp.inf, F32), jnp.full((TOPK, LANES), neg_inf, F32))
                     for _ in loads)
        return [out for _, out in lax.fori_loop(0, TOPK, rnd, init)]

    def extract_all(loads):
        outs = []
        for i in range(0, len(loads), EXTRACT_CHAINS):
            outs += extract(loads[i:i + EXTRACT_CHAINS])
        return outs

    for g in range(2 * heads):
        sc[g] = lax.dot_general(keys[g], q[:, g * dk:(g + 1) * dk], (((1,), (1,)), ((), ())),
                                preferred_element_type=F32)

    row8 = lax.broadcasted_iota(jnp.int32, (SUBLANES, LANES), 0)
    cols = [slice(c * LANES, (c + 1) * LANES) for c in range(tt // LANES)]
    for h in range(heads):
        hs = slice(h * n_keys, (h + 1) * n_keys)
        hp = slice(h * n_keys // 2, (h + 1) * n_keys // 2)
        groups = [(g, cs) for g in (2 * h, 2 * h + 1) for cs in cols]
        tops = extract_all([functools.partial(lambda g, cs: sc[g, :, cs], g, cs) for g, cs in groups])
        for (g, cs), top in zip(groups, tops):
            vals[g, :, cs] = top

        for cs in cols:
            a1 = vals[2 * h, :, cs]
            a2 = vals[2 * h + 1, :, cs]
            cand[0:TOPK, cs] = a1[0:1, :] + a2
            for p in range(1, TOPK):
                nq = TOPK // (p + 1)
                cand[TOPK + (p - 1) * SUBLANES:TOPK + p * SUBLANES, cs] = jnp.where(
                    row8 < nq, a1[p:p + 1, :] + a2[0:SUBLANES, :], neg_inf)
        sums = extract_all([functools.partial(lambda cs: cand[:, cs], cs) for cs in cols])

        for cs, top in zip(cols, sums):
            a1 = vals[2 * h, :, cs]
            a2 = vals[2 * h + 1, :, cs]
            zsum = jnp.sum(jnp.exp(top - top[0:1, :]), axis=0, keepdims=True)
            thr = top[TOPK - 1:TOPK, :]
            s1 = sc[2 * h, :, cs]
            s2 = sc[2 * h + 1, :, cs]
            cnt = jnp.zeros_like(s1)
            rank = jnp.full_like(s2, float(TOPK))
            for k in range(TOPK):
                n_k = jnp.sum(jnp.where(a1[k:k + 1, :] + a2 >= thr, 1.0, 0.0), axis=0, keepdims=True)
                cnt = jnp.where(s1 == a1[k:k + 1, :], n_k, cnt)
                rank = jnp.where(s2 == a2[k:k + 1, :], float(k), rank)
            e1_ref[hs, cs] = jnp.exp(s1 - a1[0:1, :]) / zsum
            n1_ref[hs, cs] = cnt
            e2_ref[hp, cs] = pltpu.bitcast(jnp.exp(s2 - a2[0:1, :]).astype(BF16), jnp.uint32)
            r2_ref[hp, cs] = pltpu.bitcast(rank.astype(BF16), jnp.uint32)


def _peer_select(x1, nffn, wq, keys, *, tt):
    t_total, d_model = x1.shape
    groups, n_keys, _ = keys.shape
    heads = groups // 2
    fac = pl.BlockSpec((heads * n_keys, tt), lambda i: (0, i))
    fac2 = pl.BlockSpec((heads * n_keys // 2, tt), lambda i: (0, i))
    return pl.pallas_call(
        functools.partial(_peer_select_body, tt=tt, heads=heads, n_keys=n_keys),
        grid=(t_total // tt,),
        in_specs=[pl.BlockSpec((tt, d_model), lambda i: (i, 0)),
                  _const_spec(nffn.shape), _const_spec(wq.shape), _const_spec(keys.shape)],
        out_specs=[pl.BlockSpec((tt, d_model), lambda i: (i, 0)), fac, fac, fac2, fac2],
        out_shape=[jax.ShapeDtypeStruct((t_total, d_model), BF16)]
        + [jax.ShapeDtypeStruct((heads * n_keys, t_total), F32)] * 2
        + [jax.ShapeDtypeStruct((heads * n_keys // 2, t_total), jnp.uint32)] * 2,
        scratch_shapes=[pltpu.VMEM((groups, TOPK, tt), F32),
                        pltpu.VMEM((groups, n_keys, tt), F32),
                        pltpu.VMEM((TOPK + (TOPK - 1) * SUBLANES, tt), F32)],
        compiler_params=pltpu.CompilerParams(
            dimension_semantics=("parallel",), vmem_limit_bytes=VMEM_LIMIT),
        name="peer_select",
    )(x1, nffn, wq, keys)


def _peer_dense_body(xn_ref, x1_ref, e1_ref, n1_ref, e2_ref, r2_ref, u_ref, vt_ref, nfin,
                     y_ref, acc, wgt, *, tt, ni, heads, n_keys, chunk):
    e = pl.program_id(1)

    @pl.when(e == 0)
    def _():
        acc[...] = jnp.zeros_like(acc)

    assert ni % SUBLANES == 0
    xn = xn_ref[...]
    rows_c = chunk * n_keys

    def first_key_row(ref, h, cs, i_loc):
        base = h * n_keys + e * ni + i_loc // SUBLANES * SUBLANES
        return ref[pl.ds(pl.multiple_of(base, SUBLANES), SUBLANES), cs][
            i_loc % SUBLANES:i_loc % SUBLANES + 1, :]

    for k in range(ni // chunk):
        r_lo = k * rows_c
        act = lax.dot_general(u_ref[r_lo:r_lo + rows_c, :], xn, (((1,), (1,)), ((), ())),
                              preferred_element_type=F32)
        for il in range(chunk):
            i_loc = k * chunk + il
            for c in range(tt // LANES):
                cs = slice(c * LANES, (c + 1) * LANES)
                bcast = lambda ref, h: jnp.broadcast_to(
                    first_key_row(ref, h, cs, i_loc), (BF16_ROWS, LANES)).astype(BF16)
                e1b = [bcast(e1_ref, h) for h in range(heads)]
                n1b = [bcast(n1_ref, h) for h in range(heads)]
                for rb in range(n_keys // BF16_ROWS):
                    r0 = rb * BF16_ROWS
                    gate = jnp.zeros((BF16_ROWS, LANES), BF16)
                    for h in range(heads):
                        js = slice((h * n_keys + r0) // 2, (h * n_keys + r0 + BF16_ROWS) // 2)
                        e2 = pltpu.bitcast(e2_ref[js, cs], BF16)
                        r2 = pltpu.bitcast(r2_ref[js, cs], BF16)
                        gate = gate + e1b[h] * jnp.minimum(e2, jnp.maximum(n1b[h] - r2, 0.0))
                    a = act[il * n_keys + r0:il * n_keys + r0 + BF16_ROWS, cs]
                    wgt[i_loc * n_keys + r0:i_loc * n_keys + r0 + BF16_ROWS, cs] = (
                        gate * _gelu_lowp(a))
        acc[...] += jnp.dot(vt_ref[:, r_lo:r_lo + rows_c], wgt[r_lo:r_lo + rows_c, :],
                            preferred_element_type=F32)

    @pl.when(e == pl.num_programs(1) - 1)
    def _():
        y_ref[...] = _rms(x1_ref[...] + acc[...].T, nfin[...])


def _peer_dense(xn, x1, e1, n1, e2, r2, u_tab, vt_tab, nfin, *, tt, ni, heads):
    t_total, d_model = x1.shape
    n_keys = e1.shape[0] // heads
    te = ni * n_keys
    n_experts = u_tab.shape[0]
    tok = lambda shape: pl.BlockSpec(shape, lambda t, e: (t, 0))
    fac = pl.BlockSpec((heads * n_keys, tt), lambda t, e: (0, t))
    fac2 = pl.BlockSpec((heads * n_keys // 2, tt), lambda t, e: (0, t))
    tab = pl.BlockSpec((te, d_model), lambda t, e: (e, 0))
    tab_t = pl.BlockSpec((d_model, te), lambda t, e: (0, e))
    return pl.pallas_call(
        functools.partial(_peer_dense_body, tt=tt, ni=ni, heads=heads, n_keys=n_keys, chunk=ni),
        grid=(t_total // tt, n_experts // te),
        in_specs=[tok((tt, d_model)), tok((tt, d_model)), fac, fac, fac2, fac2, tab, tab_t,
                  pl.BlockSpec(nfin.shape, lambda t, e: (0, 0))],
        out_specs=tok((tt, d_model)),
        out_shape=jax.ShapeDtypeStruct((t_total, d_model), F32),
        scratch_shapes=[pltpu.VMEM((d_model, tt), F32),
                        pltpu.VMEM((te, tt), BF16)],
        compiler_params=pltpu.CompilerParams(
            dimension_semantics=("parallel", "arbitrary"), vmem_limit_bytes=VMEM_LIMIT),
        name="peer_dense",
    )(xn, x1, e1, n1, e2, r2, u_tab, vt_tab, nfin)


def _layer(x_tm, s5re0, s5im0, lruh0, conv0, w, *, nb, tl, cw, tt, ni):
    x1, s5re, s5im, lruh, convo = _mixer(x_tm, s5re0, s5im0, lruh0, conv0, w, nb=nb, tl=tl, cw=cw)
    xn, e1, n1, e2, r2 = _peer_select(x1, w["nffn"], w["wq"], w["keys"], tt=tt)
    y = _peer_dense(xn, x1, e1, n1, e2, r2, w["u"], w["vt"], w["nfin"], tt=tt, ni=ni,
                    heads=w["keys"].shape[0] // 2)
    return y, s5re, s5im, lruh, convo


def kernel(x_prompt, x_sample, state_s5_re, state_s5_im, state_lru_h, state_conv, w_in, s5_lam_re, s5_lam_im, s5_log_dt, s5_b_re, s5_b_im, s5_c_re, s5_c_im, s5_d, s5_w_glu, s5_b_glu, conv_w, conv_b, lru_w_a, lru_b_a, lru_w_x, lru_b_x, lru_lam, w_proj_a, w_proj_b, w_out, norm_mix, norm_ffn, peer_w_q, peer_keys, peer_u, peer_v, norm_final):
    depth = w_in.shape[0]
    assert depth == 1, "single trunk layer"
    bp, seq, d_model = x_prompt.shape
    bs, dec_seq, _ = x_sample.shape
    assert dec_seq == 1
    groups, n_state_g = s5_lam_re.shape[1:]
    g_ch = s5_b_re.shape[-1]
    d_lru = lru_lam.shape[1] * lru_lam.shape[2]
    heads, _, n_keys, dk = peer_keys.shape[1:]
    n_state = groups * n_state_g
    row = lambda a: a.reshape(1, -1).astype(F32)

    abar_re, abar_im, bb_re, bb_im = _s5_discretise(
        s5_lam_re[0], s5_lam_im[0], s5_log_dt[0], s5_b_re[0], s5_b_im[0])
    bb = lambda a: _block_diag(a.reshape(groups, g_ch, n_state_g))
    cmat = lambda c: _block_diag(jnp.transpose(c, (0, 2, 1)))
    w = dict(
        nmix=row(norm_mix[0]), win=w_in[0].astype(BF16),
        bmat=jnp.concatenate([bb(bb_re), bb(bb_im)], axis=1).astype(BF16),
        abre=abar_re[::g_ch].reshape(1, n_state), abim=abar_im[::g_ch].reshape(1, n_state),
        cre=cmat(s5_c_re[0]).astype(BF16), cim=cmat(s5_c_im[0]).astype(BF16),
        dskip=row(s5_d[0]), wglu=s5_w_glu[0].astype(BF16), bglu=row(s5_b_glu[0]),
        convw=conv_w[0], convb=row(conv_b[0]),
        wax=jnp.concatenate([_block_diag(lru_w_a[0]), _block_diag(lru_w_x[0])], axis=1).astype(BF16),
        bax=jnp.concatenate([row(lru_b_a[0]), row(lru_b_x[0])], axis=1),
        lam=row(lru_lam[0]), wpa=w_proj_a[0].astype(BF16), wpb=w_proj_b[0].astype(BF16),
        wout=w_out[0].astype(BF16), nffn=row(norm_ffn[0]), wq=peer_w_q[0].astype(BF16),
        keys=peer_keys[0].reshape(2 * heads, n_keys, dk).astype(BF16),
        u=peer_u[0].astype(BF16), vt=peer_v[0].T.astype(BF16), nfin=row(norm_final),
    )

    xp_tm = jnp.transpose(x_prompt, (1, 0, 2)).reshape(seq * bp, d_model)
    zeros = lambda *s: jnp.zeros(s, F32)
    yp, p_re, p_im, p_h, p_conv = _layer(
        xp_tm, zeros(bp, n_state), zeros(bp, n_state), zeros(bp, d_lru),
        zeros((CONV_W - 1) * bp, d_lru), w, nb=bp, tl=64, cw=4 * LANES, tt=512, ni=2 * SUBLANES)
    y_prompt = jnp.transpose(yp.reshape(seq, bp, d_model), (1, 0, 2))

    conv_tm = jnp.transpose(state_conv[0], (1, 0, 2)).reshape((CONV_W - 1) * bs, d_lru)
    ys, s_re, s_im, s_h, s_conv = _layer(
        x_sample.reshape(bs, d_model), state_s5_re[0].reshape(bs, n_state),
        state_s5_im[0].reshape(bs, n_state), state_lru_h[0], conv_tm, w,
        nb=bs, tl=1, cw=LANES, tt=LANES, ni=SUBLANES)
    y_sample = ys.reshape(bs, 1, d_model)

    st = lambda a, b: a.reshape(1, b, groups, n_state_g)
    cv = lambda a, b: jnp.transpose(a.reshape(CONV_W - 1, b, d_lru), (1, 0, 2))[None]
    return (y_prompt, y_sample,
            st(p_re, bp), st(p_im, bp), p_h[None], cv(p_conv, bp),
            st(s_re, bs), st(s_im, bs), s_h[None], cv(s_conv, bs))
```

```python
import functools
import math

import jax
import jax.numpy as jnp
from jax import lax
from jax.experimental import pallas as pl
from jax.experimental.pallas import tpu as pltpu

F32 = jnp.float32
BF16 = jnp.bfloat16

EPS = 1e-6
LRU_C = 8.0
CONV_W = 4
LANES = 128
SUBLANES = 8
BF16_ROWS = 16
TOPK = 16
EXTRACT_CHAINS = 4
DIAG_SPLIT = 2
VMEM_LIMIT = 56 * 1024 * 1024


def _gelu(x):
    c = math.sqrt(2.0 / math.pi)
    return x * (0.5 * (1.0 + jnp.tanh(c * (x + 0.044715 * (x * x * x)))))


def _gelu_lowp(x):
    c = math.sqrt(2.0 / math.pi)
    inner = x * (c + (c * 0.044715) * (x * x))
    return x.astype(BF16) * (0.5 + 0.5 * jnp.tanh(inner.astype(BF16)))


def _sigmoid(x):
    return 1.0 / (1.0 + jnp.exp(-x))


def _rms(x, g):
    return x * lax.rsqrt(jnp.mean(x * x, axis=-1, keepdims=True) + EPS) * g


def _dot(a, b):
    return jnp.dot(a.astype(BF16), b, preferred_element_type=F32)


def _diag_dot(x, w_ref, col0, width, out_ref=None):
    k = x.shape[-1] // DIAG_SPLIT
    n = width // DIAG_SPLIT
    parts = []
    for b in range(DIAG_SPLIT):
        cols = slice(col0 + b * n, col0 + (b + 1) * n)
        part = _dot(x[:, b * k:(b + 1) * k], w_ref[b * k:(b + 1) * k, cols])
        if out_ref is None:
            parts.append(part)
        else:
            out_ref[:, cols] = part
    return None if out_ref is not None else jnp.concatenate(parts, axis=-1)


def _const_spec(shape):
    nd = len(shape)
    return pl.BlockSpec(shape, lambda *_: (0,) * nd, pipeline_mode=pl.Buffered(1))


def _s5_disc_body(lr_ref, li_ref, ldt_ref, br_ref, bi_ref,
                  are_ref, aim_ref, bbr_ref, bbi_ref):
    lr = lr_ref[...]
    li = li_ref[...]
    dt = jnp.exp(ldt_ref[...])
    mag = jnp.exp(lr * dt)
    a_re = mag * jnp.cos(li * dt)
    a_im = mag * jnp.sin(li * dt)
    nr = a_re - 1.0
    ni = a_im
    den = lr * lr + li * li
    f_re = (nr * lr + ni * li) / den
    f_im = (ni * lr - nr * li) / den
    br = br_ref[...]
    bi = bi_ref[...]
    are_ref[...] = a_re
    aim_ref[...] = a_im
    bbr_ref[...] = f_re * br - f_im * bi
    bbi_ref[...] = f_re * bi + f_im * br


def _s5_discretise(lam_re, lam_im, log_dt, b_re, b_im):
    g, p, h = b_re.shape
    rep = lambda a: jnp.repeat(a, h, axis=0)
    tr = lambda b: jnp.transpose(b, (0, 2, 1)).reshape(g * h, p)
    shp = jax.ShapeDtypeStruct((g * h, p), F32)
    return pl.pallas_call(
        _s5_disc_body,
        out_shape=(shp, shp, shp, shp),
        name="s5_discretise",
    )(rep(lam_re), rep(lam_im), rep(log_dt[:, None]), tr(b_re), tr(b_im))


def _block_diag(blocks):
    g, r, c = blocks.shape
    eye = jnp.eye(g, dtype=blocks.dtype)
    return (blocks[:, :, None, :] * eye[:, None, :, None]).reshape(g * r, g * c)


def _mixer_body(x_ref, s5re0, s5im0, lruh0, conv0,
                nmix, win, bmat, abre, abim, cre, cim, dskip, wglu, bglu,
                convw, convb, wax, bax, lam, wpa, wpb, wout,
                x1_ref, s5re, s5im, lruh, convo,
                proj, bu, la, lb, xbuf, xs=None, *, nb, tl, cw):
    rows = nb * tl
    d_s5 = dskip.shape[-1]
    d_lru = lam.shape[-1]
    d_model = x_ref.shape[-1]
    n_state = abre.shape[-1]
    tail = (CONV_W - 1) * nb

    @pl.when(pl.program_id(0) == 0)
    def _():
        s5re[...] = s5re0[...]
        s5im[...] = s5im0[...]
        lruh[...] = lruh0[...]
        xbuf[0:tail, :] = conv0[...]

    if xs is None:
        x = x_ref[...]
    else:
        for b in range(nb):
            for k in range(d_model // LANES):
                xs[k, pl.ds(b, tl, stride=nb), :] = x_ref[b, :, k * LANES:(k + 1) * LANES]
        x = jnp.concatenate([xs[k] for k in range(d_model // LANES)], axis=-1)
    proj[...] = _dot(_rms(x, nmix[...]), win[...])
    c0, c1, c2, c3 = d_s5, d_s5 + d_lru, d_s5 + 2 * d_lru, d_s5 + 2 * d_lru + d_model

    u = proj[:, 0:c0]
    _diag_dot(u, bmat, 0, n_state, out_ref=bu)
    _diag_dot(u, bmat, n_state, n_state, out_ref=bu)
    for c in range(n_state // cw):
        lo = c * cw
        ar = jnp.broadcast_to(abre[:, lo:lo + cw], (nb, cw))
        ai = jnp.broadcast_to(abim[:, lo:lo + cw], (nb, cw))

        def s5_step(t, carry, lo=lo, ar=ar, ai=ai):
            hr, hi = carry
            r0 = pl.multiple_of(t * nb, nb)
            br = bu[pl.ds(r0, nb), lo:lo + cw]
            bi = bu[pl.ds(r0, nb), n_state + lo:n_state + lo + cw]
            nr = ar * hr - ai * hi + br
            ni = ar * hi + ai * hr + bi
            bu[pl.ds(r0, nb), lo:lo + cw] = nr
            bu[pl.ds(r0, nb), n_state + lo:n_state + lo + cw] = ni
            return nr, ni

        hr, hi = lax.fori_loop(0, tl, s5_step, (s5re[:, lo:lo + cw], s5im[:, lo:lo + cw]),
                               unroll=min(tl, 8))
        s5re[:, lo:lo + cw] = hr
        s5im[:, lo:lo + cw] = hi

    y = (_diag_dot(bu[:, 0:n_state], cre, 0, d_s5)
         - _diag_dot(bu[:, n_state:2 * n_state], cim, 0, d_s5))
    y = y + dskip[...] * u
    z = _gelu(y)
    out_a = z * _sigmoid(_dot(z, wglu[...]) + bglu[...])

    xbuf[tail:tail + rows, :] = proj[:, c0:c1]
    acc = xbuf[0:rows, :] * convw[0:1, :]
    for k in range(1, CONV_W):
        acc = acc + xbuf[k * nb:k * nb + rows, :] * convw[k:k + 1, :]
    xc = convb[...] + acc
    new_tail = xbuf[tl * nb:tl * nb + tail, :]
    convo[...] = new_tail
    xbuf[0:tail, :] = new_tail

    r_gate = _sigmoid(_diag_dot(xc, wax, 0, d_lru) + bax[:, 0:d_lru])
    i_gate = _sigmoid(_diag_dot(xc, wax, d_lru, d_lru) + bax[:, d_lru:2 * d_lru])
    neg_lam = -lam[...]
    softplus = jnp.maximum(neg_lam, 0.0) + jnp.log1p(jnp.exp(-jnp.abs(neg_lam)))
    log_a = (-LRU_C * r_gate) * softplus
    a_t = jnp.exp(log_a)
    la[...] = a_t
    lb[...] = jnp.sqrt(-jnp.tanh(log_a) * (a_t * a_t + 1.0)) * (i_gate * xc)

    def lru_step(t, h):
        r0 = pl.multiple_of(t * nb, nb)
        hn = la[pl.ds(r0, nb), :] * h + lb[pl.ds(r0, nb), :]
        lb[pl.ds(r0, nb), :] = hn
        return hn

    lruh[...] = lax.fori_loop(0, tl, lru_step, lruh[...], unroll=min(tl, 8))
    out_b = lb[...] * _gelu(proj[:, c1:c2])

    merged = (_sigmoid(proj[:, c2:c3]) * _dot(out_a, wpa[...])
              + _sigmoid(proj[:, c3:c3 + d_model]) * _dot(out_b, wpb[...]))
    x1_ref[...] = x + _dot(merged, wout[...])


def _mixer(x_in, s5re0, s5im0, lruh0, conv0, w, *, nb, tl, cw):
    batch_major = x_in.ndim == 3
    d_model = x_in.shape[-1]
    rows_total = x_in.shape[0] * x_in.shape[1] if batch_major else x_in.shape[0]
    rows = nb * tl
    n_state = s5re0.shape[-1]
    d_lru = lruh0.shape[-1]
    d_in = w["win"].shape[-1]
    tail = (CONV_W - 1) * nb
    weights = (w["nmix"], w["win"], w["bmat"], w["abre"], w["abim"], w["cre"], w["cim"],
               w["dskip"], w["wglu"], w["bglu"], w["convw"], w["convb"], w["wax"], w["bax"],
               w["lam"], w["wpa"], w["wpb"], w["wout"])
    states = (s5re0, s5im0, lruh0, conv0)
    row_spec = pl.BlockSpec((rows, d_model), lambda i: (i, 0))
    x_spec = pl.BlockSpec((nb, tl, d_model), lambda i: (0, i, 0)) if batch_major else row_spec
    slabs = [pltpu.VMEM((d_model // LANES, rows, LANES), F32)] if batch_major else []
    return pl.pallas_call(
        functools.partial(_mixer_body, nb=nb, tl=tl, cw=cw),
        grid=(rows_total // rows,),
        in_specs=[x_spec] + [_const_spec(a.shape) for a in states + weights],
        out_specs=[row_spec] + [_const_spec(a.shape) for a in states],
        out_shape=[jax.ShapeDtypeStruct((rows_total, d_model), F32)]
        + [jax.ShapeDtypeStruct(a.shape, F32) for a in states],
        scratch_shapes=[
            pltpu.VMEM((rows, d_in), F32),
            pltpu.VMEM((rows, 2 * n_state), F32),
            pltpu.VMEM((rows, d_lru), F32),
            pltpu.VMEM((rows, d_lru), F32),
            pltpu.VMEM((rows + tail, d_lru), F32),
        ] + slabs,
        compiler_params=pltpu.CompilerParams(
            dimension_semantics=("arbitrary",), vmem_limit_bytes=VMEM_LIMIT),
        name="mixer",
    )(x_in, *states, *weights)


def _peer_select_body(x1_ref, nffn, wq, keys, xn_ref, e1_ref, n1_ref, e2_ref, r2_ref,
                      vals, sc, cand, *, tt, heads, n_keys):
    neg_inf = -jnp.inf
    xn = _rms(x1_ref[...], nffn[...]).astype(BF16)
    xn_ref[...] = xn
    q = jnp.dot(xn, wq[...], preferred_element_type=F32).astype(BF16)
    dk = keys.shape[-1]
    rowv = lax.broadcasted_iota(jnp.int32, (TOPK, LANES), 0)

    def extract(loads):
        def rnd(k, carry):
            nxt = []
            for load, (m_prev, out) in zip(loads, carry):
                s = load()
                m = jnp.max(jnp.where(s < m_prev, s, neg_inf), axis=0, keepdims=True)
                nxt.append((m, jnp.where(rowv == k, m, out)))
            return tuple(nxt)
        init = tuple((jnp.full((1, LANES), jnp.inf, F32), jnp.full((TOPK, LANES), neg_inf, F32))
                     for _ in loads)
        return [out for _, out in lax.fori_loop(0, TOPK, rnd, init)]

    def extract_all(loads):
        outs = []
        for i in range(0, len(loads), EXTRACT_CHAINS):
            outs += extract(loads[i:i + EXTRACT_CHAINS])
        return outs

    for g in range(2 * heads):
        sc[g] = lax.dot_general(keys[g], q[:, g * dk:(g + 1) * dk], (((1,), (1,)), ((), ())),
                                preferred_element_type=F32)

    row8 = lax.broadcasted_iota(jnp.int32, (SUBLANES, LANES), 0)
    cols = [slice(c * LANES, (c + 1) * LANES) for c in range(tt // LANES)]
    for h in range(heads):
        hs = slice(h * n_keys, (h + 1) * n_keys)
        hp = slice(h * n_keys // 2, (h + 1) * n_keys // 2)
        groups = [(g, cs) for g in (2 * h, 2 * h + 1) for cs in cols]
        tops = extract_all([functools.partial(lambda g, cs: sc[g, :, cs], g, cs) for g, cs in groups])
        for (g, cs), top in zip(groups, tops):
            vals[g, :, cs] = top

        for cs in cols:
            a1 = vals[2 * h, :, cs]
            a2 = vals[2 * h + 1, :, cs]
            cand[0:TOPK, cs] = a1[0:1, :] + a2
            for p in range(1, TOPK):
                nq = TOPK // (p + 1)
                cand[TOPK + (p - 1) * SUBLANES:TOPK + p * SUBLANES, cs] = jnp.where(
                    row8 < nq, a1[p:p + 1, :] + a2[0:SUBLANES, :], neg_inf)
        sums = extract_all([functools.partial(lambda cs: cand[:, cs], cs) for cs in cols])

        for cs, top in zip(cols, sums):
            a1 = vals[2 * h, :, cs]
            a2 = vals[2 * h + 1, :, cs]
            zsum = jnp.sum(jnp.exp(top - top[0:1, :]), axis=0, keepdims=True)
            thr = top[TOPK - 1:TOPK, :]
            s1 = sc[2 * h, :, cs]
            s2 = sc[2 * h + 1, :, cs]
            cnt = jnp.zeros_like(s1)
            rank = jnp.full_like(s2, float(TOPK))
            for k in range(TOPK):
                n_k = jnp.sum(jnp.where(a1[k:k + 1, :] + a2 >= thr, 1.0, 0.0), axis=0, keepdims=True)
                cnt = jnp.where(s1 == a1[k:k + 1, :], n_k, cnt)
                rank = jnp.where(s2 == a2[k:k + 1, :], float(k), rank)
            e1_ref[hs, cs] = jnp.exp(s1 - a1[0:1, :]) / zsum
            n1_ref[hs, cs] = cnt
            e2_ref[hp, cs] = pltpu.bitcast(jnp.exp(s2 - a2[0:1, :]).astype(BF16), jnp.uint32)
            r2_ref[hp, cs] = pltpu.bitcast(rank.astype(BF16), jnp.uint32)


def _peer_select(x1, nffn, wq, keys, *, tt):
    t_total, d_model = x1.shape
    groups, n_keys, _ = keys.shape
    heads = groups // 2
    fac = pl.BlockSpec((heads * n_keys, tt), lambda i: (0, i))
    fac2 = pl.BlockSpec((heads * n_keys // 2, tt), lambda i: (0, i))
    return pl.pallas_call(
        functools.partial(_peer_select_body, tt=tt, heads=heads, n_keys=n_keys),
        grid=(t_total // tt,),
        in_specs=[pl.BlockSpec((tt, d_model), lambda i: (i, 0)),
                  _const_spec(nffn.shape), _const_spec(wq.shape), _const_spec(keys.shape)],
        out_specs=[pl.BlockSpec((tt, d_model), lambda i: (i, 0)), fac, fac, fac2, fac2],
        out_shape=[jax.ShapeDtypeStruct((t_total, d_model), BF16)]
        + [jax.ShapeDtypeStruct((heads * n_keys, t_total), F32)] * 2
        + [jax.ShapeDtypeStruct((heads * n_keys // 2, t_total), jnp.uint32)] * 2,
        scratch_shapes=[pltpu.VMEM((groups, TOPK, tt), F32),
                        pltpu.VMEM((groups, n_keys, tt), F32),
                        pltpu.VMEM((TOPK + (TOPK - 1) * SUBLANES, tt), F32)],
        compiler_params=pltpu.CompilerParams(
            dimension_semantics=("parallel",), vmem_limit_bytes=VMEM_LIMIT),
        name="peer_select",
    )(x1, nffn, wq, keys)


def _peer_dense_body(xn_ref, x1_ref, e1_ref, n1_ref, e2_ref, r2_ref, u_ref, vt_ref, nfin,
                     y_ref, acc, wgt, ys, *, tt, ni, heads, n_keys, chunk, nb):
    e = pl.program_id(1)

    @pl.when(e == 0)
    def _():
        acc[...] = jnp.zeros_like(acc)

    assert ni % SUBLANES == 0
    xn = xn_ref[...]
    rows_c = chunk * n_keys

    def first_key_row(ref, h, cs, i_loc):
        base = h * n_keys + e * ni + i_loc // SUBLANES * SUBLANES
        return ref[pl.ds(pl.multiple_of(base, SUBLANES), SUBLANES), cs][
            i_loc % SUBLANES:i_loc % SUBLANES + 1, :]

    for k in range(ni // chunk):
        r_lo = k * rows_c
        act = lax.dot_general(u_ref[r_lo:r_lo + rows_c, :], xn, (((1,), (1,)), ((), ())),
                              preferred_element_type=F32)
        for il in range(chunk):
            i_loc = k * chunk + il
            for c in range(tt // LANES):
                cs = slice(c * LANES, (c + 1) * LANES)
                bcast = lambda ref, h: jnp.broadcast_to(
                    first_key_row(ref, h, cs, i_loc), (BF16_ROWS, LANES)).astype(BF16)
                e1b = [bcast(e1_ref, h) for h in range(heads)]
                n1b = [bcast(n1_ref, h) for h in range(heads)]
                for rb in range(n_keys // BF16_ROWS):
                    r0 = rb * BF16_ROWS
                    gate = jnp.zeros((BF16_ROWS, LANES), BF16)
                    for h in range(heads):
                        js = slice((h * n_keys + r0) // 2, (h * n_keys + r0 + BF16_ROWS) // 2)
                        e2 = pltpu.bitcast(e2_ref[js, cs], BF16)
                        r2 = pltpu.bitcast(r2_ref[js, cs], BF16)
                        gate = gate + e1b[h] * jnp.minimum(e2, jnp.maximum(n1b[h] - r2, 0.0))
                    a = act[il * n_keys + r0:il * n_keys + r0 + BF16_ROWS, cs]
                    wgt[i_loc * n_keys + r0:i_loc * n_keys + r0 + BF16_ROWS, cs] = (
                        gate * _gelu_lowp(a))
        acc[...] += jnp.dot(vt_ref[:, r_lo:r_lo + rows_c], wgt[r_lo:r_lo + rows_c, :],
                            preferred_element_type=F32)

    @pl.when(e == pl.num_programs(1) - 1)
    def _():
        y = _rms(x1_ref[...] + acc[...].T, nfin[...])
        if nb is None:
            y_ref[...] = y
        else:
            for k in range(y.shape[-1] // LANES):
                ys[k] = y[:, k * LANES:(k + 1) * LANES]
            for b in range(nb):
                for k in range(y.shape[-1] // LANES):
                    y_ref[b, :, k * LANES:(k + 1) * LANES] = ys[k, pl.ds(b, tt // nb, stride=nb), :]


def _peer_dense(xn, x1, e1, n1, e2, r2, u_tab, vt_tab, nfin, *, tt, ni, heads, nb=None):
    t_total, d_model = x1.shape
    n_keys = e1.shape[0] // heads
    te = ni * n_keys
    n_experts = u_tab.shape[0]
    tok = lambda shape: pl.BlockSpec(shape, lambda t, e: (t, 0))
    fac = pl.BlockSpec((heads * n_keys, tt), lambda t, e: (0, t))
    fac2 = pl.BlockSpec((heads * n_keys // 2, tt), lambda t, e: (0, t))
    tab = pl.BlockSpec((te, d_model), lambda t, e: (e, 0))
    tab_t = pl.BlockSpec((d_model, te), lambda t, e: (0, e))
    return pl.pallas_call(
        functools.partial(_peer_dense_body, tt=tt, ni=ni, heads=heads, n_keys=n_keys, chunk=ni, nb=nb),
        grid=(t_total // tt, n_experts // te),
        in_specs=[tok((tt, d_model)), tok((tt, d_model)), fac, fac, fac2, fac2, tab, tab_t,
                  pl.BlockSpec(nfin.shape, lambda t, e: (0, 0))],
        out_specs=(tok((tt, d_model)) if nb is None else
                   pl.BlockSpec((nb, tt // nb, d_model), lambda t, e: (0, t, 0))),
        out_shape=jax.ShapeDtypeStruct(
            (t_total, d_model) if nb is None else (nb, t_total // nb, d_model), F32),
        scratch_shapes=[pltpu.VMEM((d_model, tt), F32),
                        pltpu.VMEM((te, tt), BF16),
                        pltpu.VMEM((d_model // LANES, tt, LANES), F32)],
        compiler_params=pltpu.CompilerParams(
            dimension_semantics=("parallel", "arbitrary"), vmem_limit_bytes=VMEM_LIMIT),
        name="peer_dense",
    )(xn, x1, e1, n1, e2, r2, u_tab, vt_tab, nfin)


def _layer(x_tm, s5re0, s5im0, lruh0, conv0, w, *, nb, tl, cw, tt, ni, batch_major_out=False):
    x1, s5re, s5im, lruh, convo = _mixer(x_tm, s5re0, s5im0, lruh0, conv0, w, nb=nb, tl=tl, cw=cw)
    xn, e1, n1, e2, r2 = _peer_select(x1, w["nffn"], w["wq"], w["keys"], tt=tt)
    y = _peer_dense(xn, x1, e1, n1, e2, r2, w["u"], w["vt"], w["nfin"], tt=tt, ni=ni,
                    heads=w["keys"].shape[0] // 2, nb=nb if batch_major_out else None)
    return y, s5re, s5im, lruh, convo


def kernel(x_prompt, x_sample, state_s5_re, state_s5_im, state_lru_h, state_conv, w_in, s5_lam_re, s5_lam_im, s5_log_dt, s5_b_re, s5_b_im, s5_c_re, s5_c_im, s5_d, s5_w_glu, s5_b_glu, conv_w, conv_b, lru_w_a, lru_b_a, lru_w_x, lru_b_x, lru_lam, w_proj_a, w_proj_b, w_out, norm_mix, norm_ffn, peer_w_q, peer_keys, peer_u, peer_v, norm_final):
    depth = w_in.shape[0]
    assert depth == 1, "single trunk layer"
    bp, seq, d_model = x_prompt.shape
    bs, dec_seq, _ = x_sample.shape
    assert dec_seq == 1
    groups, n_state_g = s5_lam_re.shape[1:]
    g_ch = s5_b_re.shape[-1]
    d_lru = lru_lam.shape[1] * lru_lam.shape[2]
    heads, _, n_keys, dk = peer_keys.shape[1:]
    n_state = groups * n_state_g
    row = lambda a: a.reshape(1, -1).astype(F32)

    abar_re, abar_im, bb_re, bb_im = _s5_discretise(
        s5_lam_re[0], s5_lam_im[0], s5_log_dt[0], s5_b_re[0], s5_b_im[0])
    bb = lambda a: _block_diag(a.reshape(groups, g_ch, n_state_g))
    cmat = lambda c: _block_diag(jnp.transpose(c, (0, 2, 1)))
    w = dict(
        nmix=row(norm_mix[0]), win=w_in[0].astype(BF16),
        bmat=jnp.concatenate([bb(bb_re), bb(bb_im)], axis=1).astype(BF16),
        abre=abar_re[::g_ch].reshape(1, n_state), abim=abar_im[::g_ch].reshape(1, n_state),
        cre=cmat(s5_c_re[0]).astype(BF16), cim=cmat(s5_c_im[0]).astype(BF16),
        dskip=row(s5_d[0]), wglu=s5_w_glu[0].astype(BF16), bglu=row(s5_b_glu[0]),
        convw=conv_w[0], convb=row(conv_b[0]),
        wax=jnp.concatenate([_block_diag(lru_w_a[0]), _block_diag(lru_w_x[0])], axis=1).astype(BF16),
        bax=jnp.concatenate([row(lru_b_a[0]), row(lru_b_x[0])], axis=1),
        lam=row(lru_lam[0]), wpa=w_proj_a[0].astype(BF16), wpb=w_proj_b[0].astype(BF16),
        wout=w_out[0].astype(BF16), nffn=row(norm_ffn[0]), wq=peer_w_q[0].astype(BF16),
        keys=peer_keys[0].reshape(2 * heads, n_keys, dk).astype(BF16),
        u=peer_u[0].astype(BF16), vt=peer_v[0].T.astype(BF16), nfin=row(norm_final),
    )

    zeros = lambda *s: jnp.zeros(s, F32)
    yp, p_re, p_im, p_h, p_conv = _layer(
        x_prompt, zeros(bp, n_state), zeros(bp, n_state), zeros(bp, d_lru),
        zeros((CONV_W - 1) * bp, d_lru), w, nb=bp, tl=64, cw=4 * LANES, tt=512, ni=2 * SUBLANES, batch_major_out=True)
    y_prompt = yp

    conv_tm = jnp.transpose(state_conv[0], (1, 0, 2)).reshape((CONV_W - 1) * bs, d_lru)
    ys, s_re, s_im, s_h, s_conv = _layer(
        x_sample.reshape(bs, d_model), state_s5_re[0].reshape(bs, n_state),
        state_s5_im[0].reshape(bs, n_state), state_lru_h[0], conv_tm, w,
        nb=bs, tl=1, cw=LANES, tt=LANES, ni=SUBLANES)
    y_sample = ys.reshape(bs, 1, d_model)

    st = lambda a, b: a.reshape(1, b, groups, n_state_g)
    cv = lambda a, b: jnp.transpose(a.reshape(CONV_W - 1, b, d_lru), (1, 0, 2))[None]
    return (y_prompt, y_sample,
            st(p_re, bp), st(p_im, bp), p_h[None], cv(p_conv, bp),
            st(s_re, bs), st(s_im, bs), s_h[None], cv(s_conv, bs))
```

```python
import functools
import math

import jax
import jax.numpy as jnp
from jax import lax
from jax.experimental import pallas as pl
from jax.experimental.pallas import tpu as pltpu

F32 = jnp.float32
BF16 = jnp.bfloat16

EPS = 1e-6
LRU_C = 8.0
CONV_W = 4
LANES = 128
SUBLANES = 8
BF16_ROWS = 16
TOPK = 16
EXTRACT_CHAINS = 8
DIAG_SPLIT = 2
VMEM_LIMIT = 56 * 1024 * 1024


def _gelu(x):
    c = math.sqrt(2.0 / math.pi)
    return x * (0.5 * (1.0 + jnp.tanh(c * (x + 0.044715 * (x * x * x)))))


def _gelu_lowp(x):
    c = math.sqrt(2.0 / math.pi)
    inner = x * (c + (c * 0.044715) * (x * x))
    return x.astype(BF16) * (0.5 + 0.5 * jnp.tanh(inner.astype(BF16)))


def _sigmoid(x):
    return 1.0 / (1.0 + jnp.exp(-x))


def _rms(x, g):
    return x * lax.rsqrt(jnp.mean(x * x, axis=-1, keepdims=True) + EPS) * g


def _dot(a, b):
    return jnp.dot(a.astype(BF16), b, preferred_element_type=F32)


def _diag_dot(x, w_ref, col0, width, out_ref=None):
    k = x.shape[-1] // DIAG_SPLIT
    n = width // DIAG_SPLIT
    parts = []
    for b in range(DIAG_SPLIT):
        cols = slice(col0 + b * n, col0 + (b + 1) * n)
        part = _dot(x[:, b * k:(b + 1) * k], w_ref[b * k:(b + 1) * k, cols])
        if out_ref is None:
            parts.append(part)
        else:
            out_ref[:, cols] = part
    return None if out_ref is not None else jnp.concatenate(parts, axis=-1)


def _const_spec(shape):
    nd = len(shape)
    return pl.BlockSpec(shape, lambda *_: (0,) * nd, pipeline_mode=pl.Buffered(1))


def _s5_disc_body(lr_ref, li_ref, ldt_ref, br_ref, bi_ref,
                  are_ref, aim_ref, bbr_ref, bbi_ref):
    lr = lr_ref[...]
    li = li_ref[...]
    dt = jnp.exp(ldt_ref[...])
    mag = jnp.exp(lr * dt)
    a_re = mag * jnp.cos(li * dt)
    a_im = mag * jnp.sin(li * dt)
    nr = a_re - 1.0
    ni = a_im
    den = lr * lr + li * li
    f_re = (nr * lr + ni * li) / den
    f_im = (ni * lr - nr * li) / den
    br = br_ref[...]
    bi = bi_ref[...]
    are_ref[...] = a_re
    aim_ref[...] = a_im
    bbr_ref[...] = f_re * br - f_im * bi
    bbi_ref[...] = f_re * bi + f_im * br


def _s5_discretise(lam_re, lam_im, log_dt, b_re, b_im):
    g, p, h = b_re.shape
    rep = lambda a: jnp.repeat(a, h, axis=0)
    tr = lambda b: jnp.transpose(b, (0, 2, 1)).reshape(g * h, p)
    shp = jax.ShapeDtypeStruct((g * h, p), F32)
    return pl.pallas_call(
        _s5_disc_body,
        out_shape=(shp, shp, shp, shp),
        name="s5_discretise",
    )(rep(lam_re), rep(lam_im), rep(log_dt[:, None]), tr(b_re), tr(b_im))


def _block_diag(blocks):
    g, r, c = blocks.shape
    eye = jnp.eye(g, dtype=blocks.dtype)
    return (blocks[:, :, None, :] * eye[:, None, :, None]).reshape(g * r, g * c)


def _mixer_body(x_ref, s5re0, s5im0, lruh0, conv0,
                nmix, win, bmat, abre, abim, cre, cim, dskip, wglu, bglu,
                convw, convb, wax, bax, lam, wpa, wpb, wout,
                x1_ref, s5re, s5im, lruh, convo,
                proj, bu, la, lb, xbuf, xs=None, *, nb, tl, cw):
    rows = nb * tl
    d_s5 = dskip.shape[-1]
    d_lru = lam.shape[-1]
    d_model = x_ref.shape[-1]
    n_state = abre.shape[-1]
    tail = (CONV_W - 1) * nb

    @pl.when(pl.program_id(0) == 0)
    def _():
        s5re[...] = s5re0[...]
        s5im[...] = s5im0[...]
        lruh[...] = lruh0[...]
        xbuf[0:tail, :] = conv0[...]

    if xs is None:
        x = x_ref[...]
    else:
        for b in range(nb):
            for k in range(d_model // LANES):
                xs[k, pl.ds(b, tl, stride=nb), :] = x_ref[b, :, k * LANES:(k + 1) * LANES]
        x = jnp.concatenate([xs[k] for k in range(d_model // LANES)], axis=-1)
    proj[...] = _dot(_rms(x, nmix[...]), win[...])
    c0, c1, c2, c3 = d_s5, d_s5 + d_lru, d_s5 + 2 * d_lru, d_s5 + 2 * d_lru + d_model

    u = proj[:, 0:c0]
    _diag_dot(u, bmat, 0, n_state, out_ref=bu)
    _diag_dot(u, bmat, n_state, n_state, out_ref=bu)
    for c in range(n_state // cw):
        lo = c * cw
        ar = jnp.broadcast_to(abre[:, lo:lo + cw], (nb, cw))
        ai = jnp.broadcast_to(abim[:, lo:lo + cw], (nb, cw))

        def s5_step(t, carry, lo=lo, ar=ar, ai=ai):
            hr, hi = carry
            r0 = pl.multiple_of(t * nb, nb)
            br = bu[pl.ds(r0, nb), lo:lo + cw]
            bi = bu[pl.ds(r0, nb), n_state + lo:n_state + lo + cw]
            nr = ar * hr - ai * hi + br
            ni = ar * hi + ai * hr + bi
            bu[pl.ds(r0, nb), lo:lo + cw] = nr
            bu[pl.ds(r0, nb), n_state + lo:n_state + lo + cw] = ni
            return nr, ni

        hr, hi = lax.fori_loop(0, tl, s5_step, (s5re[:, lo:lo + cw], s5im[:, lo:lo + cw]),
                               unroll=min(tl, 8))
        s5re[:, lo:lo + cw] = hr
        s5im[:, lo:lo + cw] = hi

    y = (_diag_dot(bu[:, 0:n_state], cre, 0, d_s5)
         - _diag_dot(bu[:, n_state:2 * n_state], cim, 0, d_s5))
    y = y + dskip[...] * u
    z = _gelu(y)
    out_a = z * _sigmoid(_dot(z, wglu[...]) + bglu[...])

    xbuf[tail:tail + rows, :] = proj[:, c0:c1]
    acc = xbuf[0:rows, :] * convw[0:1, :]
    for k in range(1, CONV_W):
        acc = acc + xbuf[k * nb:k * nb + rows, :] * convw[k:k + 1, :]
    xc = convb[...] + acc
    new_tail = xbuf[tl * nb:tl * nb + tail, :]
    convo[...] = new_tail
    xbuf[0:tail, :] = new_tail

    r_gate = _sigmoid(_diag_dot(xc, wax, 0, d_lru) + bax[:, 0:d_lru])
    i_gate = _sigmoid(_diag_dot(xc, wax, d_lru, d_lru) + bax[:, d_lru:2 * d_lru])
    neg_lam = -lam[...]
    softplus = jnp.maximum(neg_lam, 0.0) + jnp.log1p(jnp.exp(-jnp.abs(neg_lam)))
    log_a = (-LRU_C * r_gate) * softplus
    a_t = jnp.exp(log_a)
    la[...] = a_t
    lb[...] = jnp.sqrt(-jnp.tanh(log_a) * (a_t * a_t + 1.0)) * (i_gate * xc)

    def lru_step(t, h):
        r0 = pl.multiple_of(t * nb, nb)
        hn = la[pl.ds(r0, nb), :] * h + lb[pl.ds(r0, nb), :]
        lb[pl.ds(r0, nb), :] = hn
        return hn

    lruh[...] = lax.fori_loop(0, tl, lru_step, lruh[...], unroll=min(tl, 8))
    out_b = lb[...] * _gelu(proj[:, c1:c2])

    merged = (_sigmoid(proj[:, c2:c3]) * _dot(out_a, wpa[...])
              + _sigmoid(proj[:, c3:c3 + d_model]) * _dot(out_b, wpb[...]))
    x1_ref[...] = x + _dot(merged, wout[...])


def _mixer(x_in, s5re0, s5im0, lruh0, conv0, w, *, nb, tl, cw):
    batch_major = x_in.ndim == 3
    d_model = x_in.shape[-1]
    rows_total = x_in.shape[0] * x_in.shape[1] if batch_major else x_in.shape[0]
    rows = nb * tl
    n_state = s5re0.shape[-1]
    d_lru = lruh0.shape[-1]
    d_in = w["win"].shape[-1]
    tail = (CONV_W - 1) * nb
    weights = (w["nmix"], w["win"], w["bmat"], w["abre"], w["abim"], w["cre"], w["cim"],
               w["dskip"], w["wglu"], w["bglu"], w["convw"], w["convb"], w["wax"], w["bax"],
               w["lam"], w["wpa"], w["wpb"], w["wout"])
    states = (s5re0, s5im0, lruh0, conv0)
    row_spec = pl.BlockSpec((rows, d_model), lambda i: (i, 0))
    x_spec = pl.BlockSpec((nb, tl, d_model), lambda i: (0, i, 0)) if batch_major else row_spec
    slabs = [pltpu.VMEM((d_model // LANES, rows, LANES), F32)] if batch_major else []
    return pl.pallas_call(
        functools.partial(_mixer_body, nb=nb, tl=tl, cw=cw),
        grid=(rows_total // rows,),
        in_specs=[x_spec] + [_const_spec(a.shape) for a in states + weights],
        out_specs=[row_spec] + [_const_spec(a.shape) for a in states],
        out_shape=[jax.ShapeDtypeStruct((rows_total, d_model), F32)]
        + [jax.ShapeDtypeStruct(a.shape, F32) for a in states],
        scratch_shapes=[
            pltpu.VMEM((rows, d_in), F32),
            pltpu.VMEM((rows, 2 * n_state), F32),
            pltpu.VMEM((rows, d_lru), F32),
            pltpu.VMEM((rows, d_lru), F32),
            pltpu.VMEM((rows + tail, d_lru), F32),
        ] + slabs,
        compiler_params=pltpu.CompilerParams(
            dimension_semantics=("arbitrary",), vmem_limit_bytes=VMEM_LIMIT),
        name="mixer",
    )(x_in, *states, *weights)


def _peer_select_body(x1_ref, nffn, wq, keys, xn_ref, e1_ref, n1_ref, e2_ref, r2_ref,
                      vals, sc, cand, *, tt, heads, n_keys):
    neg_inf = -jnp.inf
    xn = _rms(x1_ref[...], nffn[...]).astype(BF16)
    xn_ref[...] = xn
    q = jnp.dot(xn, wq[...], preferred_element_type=F32).astype(BF16)
    dk = keys.shape[-1]
    rowv = lax.broadcasted_iota(jnp.int32, (TOPK, LANES), 0)

    def extract(loads):
        def rnd(k, carry):
            nxt = []
            for load, (m_prev, out) in zip(loads, carry):
                s = load()
                m = jnp.max(jnp.where(s < m_prev, s, neg_inf), axis=0, keepdims=True)
                nxt.append((m, jnp.where(rowv == k, m, out)))
            return tuple(nxt)
        init = tuple((jnp.full((1, LANES), jnp.inf, F32), jnp.full((TOPK, LANES), neg_inf, F32))
                     for _ in loads)
        return [out for _, out in lax.fori_loop(0, TOPK, rnd, init)]

    def extract_all(loads):
        outs = []
        for i in range(0, len(loads), EXTRACT_CHAINS):
            outs += extract(loads[i:i + EXTRACT_CHAINS])
        return outs

    for g in range(2 * heads):
        sc[g] = lax.dot_general(keys[g], q[:, g * dk:(g + 1) * dk], (((1,), (1,)), ((), ())),
                                preferred_element_type=F32)

    row8 = lax.broadcasted_iota(jnp.int32, (SUBLANES, LANES), 0)
    cols = [slice(c * LANES, (c + 1) * LANES) for c in range(tt // LANES)]
    for h in range(heads):
        hs = slice(h * n_keys, (h + 1) * n_keys)
        hp = slice(h * n_keys // 2, (h + 1) * n_keys // 2)
        groups = [(g, cs) for g in (2 * h, 2 * h + 1) for cs in cols]
        tops = extract_all([functools.partial(lambda g, cs: sc[g, :, cs], g, cs) for g, cs in groups])
        for (g, cs), top in zip(groups, tops):
            vals[g, :, cs] = top

        for cs in cols:
            a1 = vals[2 * h, :, cs]
            a2 = vals[2 * h + 1, :, cs]
            cand[0:TOPK, cs] = a1[0:1, :] + a2
            for p in range(1, TOPK):
                nq = TOPK // (p + 1)
                cand[TOPK + (p - 1) * SUBLANES:TOPK + p * SUBLANES, cs] = jnp.where(
                    row8 < nq, a1[p:p + 1, :] + a2[0:SUBLANES, :], neg_inf)
        sums = extract_all([functools.partial(lambda cs: cand[:, cs], cs) for cs in cols])

        for cs, top in zip(cols, sums):
            a1 = vals[2 * h, :, cs]
            a2 = vals[2 * h + 1, :, cs]
            zsum = jnp.sum(jnp.exp(top - top[0:1, :]), axis=0, keepdims=True)
            thr = top[TOPK - 1:TOPK, :]
            s1 = sc[2 * h, :, cs]
            s2 = sc[2 * h + 1, :, cs]
            cnt = jnp.zeros_like(s1)
            rank = jnp.full_like(s2, float(TOPK))
            for k in range(TOPK):
                n_k = jnp.sum(jnp.where(a1[k:k + 1, :] + a2 >= thr, 1.0, 0.0), axis=0, keepdims=True)
                cnt = jnp.where(s1 == a1[k:k + 1, :], n_k, cnt)
                rank = jnp.where(s2 == a2[k:k + 1, :], float(k), rank)
            e1_ref[hs, cs] = jnp.exp(s1 - a1[0:1, :]) / zsum
            n1_ref[hs, cs] = cnt
            e2_ref[hp, cs] = pltpu.bitcast(jnp.exp(s2 - a2[0:1, :]).astype(BF16), jnp.uint32)
            r2_ref[hp, cs] = pltpu.bitcast(rank.astype(BF16), jnp.uint32)


def _peer_select(x1, nffn, wq, keys, *, tt):
    t_total, d_model = x1.shape
    groups, n_keys, _ = keys.shape
    heads = groups // 2
    fac = pl.BlockSpec((heads * n_keys, tt), lambda i: (0, i))
    fac2 = pl.BlockSpec((heads * n_keys // 2, tt), lambda i: (0, i))
    return pl.pallas_call(
        functools.partial(_peer_select_body, tt=tt, heads=heads, n_keys=n_keys),
        grid=(t_total // tt,),
        in_specs=[pl.BlockSpec((tt, d_model), lambda i: (i, 0)),
                  _const_spec(nffn.shape), _const_spec(wq.shape), _const_spec(keys.shape)],
        out_specs=[pl.BlockSpec((tt, d_model), lambda i: (i, 0)), fac, fac, fac2, fac2],
        out_shape=[jax.ShapeDtypeStruct((t_total, d_model), BF16)]
        + [jax.ShapeDtypeStruct((heads * n_keys, t_total), F32)] * 2
        + [jax.ShapeDtypeStruct((heads * n_keys // 2, t_total), jnp.uint32)] * 2,
        scratch_shapes=[pltpu.VMEM((groups, TOPK, tt), F32),
                        pltpu.VMEM((groups, n_keys, tt), F32),
                        pltpu.VMEM((TOPK + (TOPK - 1) * SUBLANES, tt), F32)],
        compiler_params=pltpu.CompilerParams(
            dimension_semantics=("parallel",), vmem_limit_bytes=VMEM_LIMIT),
        name="peer_select",
    )(x1, nffn, wq, keys)


def _peer_dense_body(xn_ref, x1_ref, e1_ref, n1_ref, e2_ref, r2_ref, u_ref, vt_ref, nfin,
                     y_ref, acc, wgt, ys, *, tt, ni, heads, n_keys, chunk, nb):
    e = pl.program_id(1)

    @pl.when(e == 0)
    def _():
        acc[...] = jnp.zeros_like(acc)

    assert ni % SUBLANES == 0
    xn = xn_ref[...]
    rows_c = chunk * n_keys

    def first_key_row(ref, h, cs, i_loc):
        base = h * n_keys + e * ni + i_loc // SUBLANES * SUBLANES
        return ref[pl.ds(pl.multiple_of(base, SUBLANES), SUBLANES), cs][
            i_loc % SUBLANES:i_loc % SUBLANES + 1, :]

    for k in range(ni // chunk):
        r_lo = k * rows_c
        act = lax.dot_general(u_ref[r_lo:r_lo + rows_c, :], xn, (((1,), (1,)), ((), ())),
                              preferred_element_type=F32)
        for il in range(chunk):
            i_loc = k * chunk + il
            for c in range(tt // LANES):
                cs = slice(c * LANES, (c + 1) * LANES)
                bcast = lambda ref, h: jnp.broadcast_to(
                    first_key_row(ref, h, cs, i_loc), (BF16_ROWS, LANES)).astype(BF16)
                e1b = [bcast(e1_ref, h) for h in range(heads)]
                n1b = [bcast(n1_ref, h) for h in range(heads)]
                for rb in range(n_keys // BF16_ROWS):
                    r0 = rb * BF16_ROWS
                    gate = jnp.zeros((BF16_ROWS, LANES), BF16)
                    for h in range(heads):
                        js = slice((h * n_keys + r0) // 2, (h * n_keys + r0 + BF16_ROWS) // 2)
                        e2 = pltpu.bitcast(e2_ref[js, cs], BF16)
                        r2 = pltpu.bitcast(r2_ref[js, cs], BF16)
                        gate = gate + e1b[h] * jnp.minimum(e2, jnp.maximum(n1b[h] - r2, 0.0))
                    a = act[il * n_keys + r0:il * n_keys + r0 + BF16_ROWS, cs]
                    wgt[i_loc * n_keys + r0:i_loc * n_keys + r0 + BF16_ROWS, cs] = (
                        gate * _gelu_lowp(a))
        acc[...] += jnp.dot(vt_ref[:, r_lo:r_lo + rows_c], wgt[r_lo:r_lo + rows_c, :],
                            preferred_element_type=F32)

    @pl.when(e == pl.num_programs(1) - 1)
    def _():
        y = _rms(x1_ref[...] + acc[...].T, nfin[...])
        if nb is None:
            y_ref[...] = y
        else:
            for k in range(y.shape[-1] // LANES):
                ys[k] = y[:, k * LANES:(k + 1) * LANES]
            for b in range(nb):
                for k in range(y.shape[-1] // LANES):
                    y_ref[b, :, k * LANES:(k + 1) * LANES] = ys[k, pl.ds(b, tt // nb, stride=nb), :]


def _peer_dense(xn, x1, e1, n1, e2, r2, u_tab, vt_tab, nfin, *, tt, ni, heads, nb=None):
    t_total, d_model = x1.shape
    n_keys = e1.shape[0] // heads
    te = ni * n_keys
    n_experts = u_tab.shape[0]
    tok = lambda shape: pl.BlockSpec(shape, lambda t, e: (t, 0))
    fac = pl.BlockSpec((heads * n_keys, tt), lambda t, e: (0, t))
    fac2 = pl.BlockSpec((heads * n_keys // 2, tt), lambda t, e: (0, t))
    tab = pl.BlockSpec((te, d_model), lambda t, e: (e, 0))
    tab_t = pl.BlockSpec((d_model, te), lambda t, e: (0, e))
    return pl.pallas_call(
        functools.partial(_peer_dense_body, tt=tt, ni=ni, heads=heads, n_keys=n_keys, chunk=ni, nb=nb),
        grid=(t_total // tt, n_experts // te),
        in_specs=[tok((tt, d_model)), tok((tt, d_model)), fac, fac, fac2, fac2, tab, tab_t,
                  pl.BlockSpec(nfin.shape, lambda t, e: (0, 0))],
        out_specs=(tok((tt, d_model)) if nb is None else
                   pl.BlockSpec((nb, tt // nb, d_model), lambda t, e: (0, t, 0))),
        out_shape=jax.ShapeDtypeStruct(
            (t_total, d_model) if nb is None else (nb, t_total // nb, d_model), F32),
        scratch_shapes=[pltpu.VMEM((d_model, tt), F32),
                        pltpu.VMEM((te, tt), BF16),
                        pltpu.VMEM((d_model // LANES, tt, LANES), F32)],
        compiler_params=pltpu.CompilerParams(
            dimension_semantics=("parallel", "arbitrary"), vmem_limit_bytes=VMEM_LIMIT),
        name="peer_dense",
    )(xn, x1, e1, n1, e2, r2, u_tab, vt_tab, nfin)


def _layer(x_tm, s5re0, s5im0, lruh0, conv0, w, *, nb, tl, cw, tt, ni, batch_major_out=False):
    x1, s5re, s5im, lruh, convo = _mixer(x_tm, s5re0, s5im0, lruh0, conv0, w, nb=nb, tl=tl, cw=cw)
    xn, e1, n1, e2, r2 = _peer_select(x1, w["nffn"], w["wq"], w["keys"], tt=tt)
    y = _peer_dense(xn, x1, e1, n1, e2, r2, w["u"], w["vt"], w["nfin"], tt=tt, ni=ni,
                    heads=w["keys"].shape[0] // 2, nb=nb if batch_major_out else None)
    return y, s5re, s5im, lruh, convo


def kernel(x_prompt, x_sample, state_s5_re, state_s5_im, state_lru_h, state_conv, w_in, s5_lam_re, s5_lam_im, s5_log_dt, s5_b_re, s5_b_im, s5_c_re, s5_c_im, s5_d, s5_w_glu, s5_b_glu, conv_w, conv_b, lru_w_a, lru_b_a, lru_w_x, lru_b_x, lru_lam, w_proj_a, w_proj_b, w_out, norm_mix, norm_ffn, peer_w_q, peer_keys, peer_u, peer_v, norm_final):
    depth = w_in.shape[0]
    assert depth == 1, "single trunk layer"
    bp, seq, d_model = x_prompt.shape
    bs, dec_seq, _ = x_sample.shape
    assert dec_seq == 1
    groups, n_state_g = s5_lam_re.shape[1:]
    g_ch = s5_b_re.shape[-1]
    d_lru = lru_lam.shape[1] * lru_lam.shape[2]
    heads, _, n_keys, dk = peer_keys.shape[1:]
    n_state = groups * n_state_g
    row = lambda a: a.reshape(1, -1).astype(F32)

    abar_re, abar_im, bb_re, bb_im = _s5_discretise(
        s5_lam_re[0], s5_lam_im[0], s5_log_dt[0], s5_b_re[0], s5_b_im[0])
    bb = lambda a: _block_diag(a.reshape(groups, g_ch, n_state_g))
    cmat = lambda c: _block_diag(jnp.transpose(c, (0, 2, 1)))
    w = dict(
        nmix=row(norm_mix[0]), win=w_in[0].astype(BF16),
        bmat=jnp.concatenate([bb(bb_re), bb(bb_im)], axis=1).astype(BF16),
        abre=abar_re[::g_ch].reshape(1, n_state), abim=abar_im[::g_ch].reshape(1, n_state),
        cre=cmat(s5_c_re[0]).astype(BF16), cim=cmat(s5_c_im[0]).astype(BF16),
        dskip=row(s5_d[0]), wglu=s5_w_glu[0].astype(BF16), bglu=row(s5_b_glu[0]),
        convw=conv_w[0], convb=row(conv_b[0]),
        wax=jnp.concatenate([_block_diag(lru_w_a[0]), _block_diag(lru_w_x[0])], axis=1).astype(BF16),
        bax=jnp.concatenate([row(lru_b_a[0]), row(lru_b_x[0])], axis=1),
        lam=row(lru_lam[0]), wpa=w_proj_a[0].astype(BF16), wpb=w_proj_b[0].astype(BF16),
        wout=w_out[0].astype(BF16), nffn=row(norm_ffn[0]), wq=peer_w_q[0].astype(BF16),
        keys=peer_keys[0].reshape(2 * heads, n_keys, dk).astype(BF16),
        u=peer_u[0].astype(BF16), vt=peer_v[0].T.astype(BF16), nfin=row(norm_final),
    )

    zeros = lambda *s: jnp.zeros(s, F32)
    yp, p_re, p_im, p_h, p_conv = _layer(
        x_prompt, zeros(bp, n_state), zeros(bp, n_state), zeros(bp, d_lru),
        zeros((CONV_W - 1) * bp, d_lru), w, nb=bp, tl=64, cw=8 * LANES, tt=512, ni=2 * SUBLANES, batch_major_out=True)
    y_prompt = yp

    conv_tm = jnp.transpose(state_conv[0], (1, 0, 2)).reshape((CONV_W - 1) * bs, d_lru)
    ys, s_re, s_im, s_h, s_conv = _layer(
        x_sample.reshape(bs, d_model), state_s5_re[0].reshape(bs, n_state),
        state_s5_im[0].reshape(bs, n_state), state_lru_h[0], conv_tm, w,
        nb=bs, tl=1, cw=LANES, tt=LANES, ni=2 * SUBLANES)
    y_sample = ys.reshape(bs, 1, d_model)

    st = lambda a, b: a.reshape(1, b, groups, n_state_g)
    cv = lambda a, b: jnp.transpose(a.reshape(CONV_W - 1, b, d_lru), (1, 0, 2))[None]
    return (y_prompt, y_sample,
            st(p_re, bp), st(p_im, bp), p_h[None], cv(p_conv, bp),
            st(s_re, bs), st(s_im, bs), s_h[None], cv(s_conv, bs))
```

```python
import functools
import math

import numpy as np
import jax
import jax.numpy as jnp
from jax import lax
from jax.experimental import pallas as pl
from jax.experimental.pallas import tpu as pltpu

F32 = jnp.float32
BF16 = jnp.bfloat16

EPS = 1e-6
LRU_C = 8.0
CONV_W = 4
LANES = 128
SUBLANES = 8
BF16_ROWS = 16
TOPK = 16
EXTRACT_CHAINS = 8
DIAG_SPLIT = 2
VMEM_LIMIT = 56 * 1024 * 1024


def _gelu(x):
    c = math.sqrt(2.0 / math.pi)
    return x * (0.5 * (1.0 + jnp.tanh(c * (x + 0.044715 * (x * x * x)))))


def _bf16_pair(c):
    hi = float(np.asarray(c, dtype=BF16))
    return hi, float(np.asarray(c - hi, dtype=BF16))


def _gelu_lowp(x):
    c_hi, c_lo = _bf16_pair(math.sqrt(2.0 / math.pi))
    k_hi, k_lo = _bf16_pair(math.sqrt(2.0 / math.pi) * 0.044715)
    x = x.astype(BF16)
    x2 = x * x
    inner = x * (((k_hi * x2 + k_lo * x2) + c_hi) + c_lo)
    return x * (0.5 + 0.5 * jnp.tanh(inner))


def _sigmoid(x):
    return 1.0 / (1.0 + jnp.exp(-x))


def _rms(x, g):
    return x * lax.rsqrt(jnp.mean(x * x, axis=-1, keepdims=True) + EPS) * g


def _dot(a, b):
    return jnp.dot(a.astype(BF16), b, preferred_element_type=F32)


def _diag_dot(x, w_ref, col0, width, out_ref=None):
    k = x.shape[-1] // DIAG_SPLIT
    n = width // DIAG_SPLIT
    parts = []
    for b in range(DIAG_SPLIT):
        cols = slice(col0 + b * n, col0 + (b + 1) * n)
        part = _dot(x[:, b * k:(b + 1) * k], w_ref[b * k:(b + 1) * k, cols])
        if out_ref is None:
            parts.append(part)
        else:
            out_ref[:, cols] = part
    return None if out_ref is not None else jnp.concatenate(parts, axis=-1)


def _const_spec(shape):
    nd = len(shape)
    return pl.BlockSpec(shape, lambda *_: (0,) * nd, pipeline_mode=pl.Buffered(1))


def _s5_disc_body(lr_ref, li_ref, ldt_ref, br_ref, bi_ref,
                  are_ref, aim_ref, bbr_ref, bbi_ref):
    lr = lr_ref[...]
    li = li_ref[...]
    dt = jnp.exp(ldt_ref[...])
    mag = jnp.exp(lr * dt)
    a_re = mag * jnp.cos(li * dt)
    a_im = mag * jnp.sin(li * dt)
    nr = a_re - 1.0
    ni = a_im
    den = lr * lr + li * li
    f_re = (nr * lr + ni * li) / den
    f_im = (ni * lr - nr * li) / den
    br = br_ref[...]
    bi = bi_ref[...]
    are_ref[...] = a_re
    aim_ref[...] = a_im
    bbr_ref[...] = f_re * br - f_im * bi
    bbi_ref[...] = f_re * bi + f_im * br


def _s5_discretise(lam_re, lam_im, log_dt, b_re, b_im):
    g, p, h = b_re.shape
    rep = lambda a: jnp.repeat(a, h, axis=0)
    tr = lambda b: jnp.transpose(b, (0, 2, 1)).reshape(g * h, p)
    shp = jax.ShapeDtypeStruct((g * h, p), F32)
    return pl.pallas_call(
        _s5_disc_body,
        out_shape=(shp, shp, shp, shp),
        name="s5_discretise",
    )(rep(lam_re), rep(lam_im), rep(log_dt[:, None]), tr(b_re), tr(b_im))


def _block_diag(blocks):
    g, r, c = blocks.shape
    eye = jnp.eye(g, dtype=blocks.dtype)
    return (blocks[:, :, None, :] * eye[:, None, :, None]).reshape(g * r, g * c)


def _mixer_body(x_ref, s5re0, s5im0, lruh0, conv0,
                nmix, win, bmat, abre, abim, cre, cim, dskip, wglu, bglu,
                convw, convb, wax, bax, lam, wpa, wpb, wout,
                x1_ref, s5re, s5im, lruh, convo,
                proj, bu, la, lb, xbuf, xs=None, *, nb, tl, cw):
    rows = nb * tl
    d_s5 = dskip.shape[-1]
    d_lru = lam.shape[-1]
    d_model = x_ref.shape[-1]
    n_state = abre.shape[-1]
    tail = (CONV_W - 1) * nb

    @pl.when(pl.program_id(0) == 0)
    def _():
        s5re[...] = s5re0[...]
        s5im[...] = s5im0[...]
        lruh[...] = lruh0[...]
        xbuf[0:tail, :] = conv0[...]

    if xs is None:
        x = x_ref[...]
    else:
        for b in range(nb):
            for k in range(d_model // LANES):
                xs[k, pl.ds(b, tl, stride=nb), :] = x_ref[b, :, k * LANES:(k + 1) * LANES]
        x = jnp.concatenate([xs[k] for k in range(d_model // LANES)], axis=-1)
    proj[...] = _dot(_rms(x, nmix[...]), win[...])
    c0, c1, c2, c3 = d_s5, d_s5 + d_lru, d_s5 + 2 * d_lru, d_s5 + 2 * d_lru + d_model

    u = proj[:, 0:c0]
    _diag_dot(u, bmat, 0, n_state, out_ref=bu)
    _diag_dot(u, bmat, n_state, n_state, out_ref=bu)
    for c in range(n_state // cw):
        lo = c * cw
        ar = jnp.broadcast_to(abre[:, lo:lo + cw], (nb, cw))
        ai = jnp.broadcast_to(abim[:, lo:lo + cw], (nb, cw))

        def s5_step(t, carry, lo=lo, ar=ar, ai=ai):
            hr, hi = carry
            r0 = pl.multiple_of(t * nb, nb)
            br = bu[pl.ds(r0, nb), lo:lo + cw]
            bi = bu[pl.ds(r0, nb), n_state + lo:n_state + lo + cw]
            nr = ar * hr - ai * hi + br
            ni = ar * hi + ai * hr + bi
            bu[pl.ds(r0, nb), lo:lo + cw] = nr
            bu[pl.ds(r0, nb), n_state + lo:n_state + lo + cw] = ni
            return nr, ni

        hr, hi = lax.fori_loop(0, tl, s5_step, (s5re[:, lo:lo + cw], s5im[:, lo:lo + cw]),
                               unroll=min(tl, 8))
        s5re[:, lo:lo + cw] = hr
        s5im[:, lo:lo + cw] = hi

    y = (_diag_dot(bu[:, 0:n_state], cre, 0, d_s5)
         - _diag_dot(bu[:, n_state:2 * n_state], cim, 0, d_s5))
    y = y + dskip[...] * u
    z = _gelu(y)
    out_a = z * _sigmoid(_dot(z, wglu[...]) + bglu[...])

    xbuf[tail:tail + rows, :] = proj[:, c0:c1]
    acc = xbuf[0:rows, :] * convw[0:1, :]
    for k in range(1, CONV_W):
        acc = acc + xbuf[k * nb:k * nb + rows, :] * convw[k:k + 1, :]
    xc = convb[...] + acc
    new_tail = xbuf[tl * nb:tl * nb + tail, :]
    convo[...] = new_tail
    xbuf[0:tail, :] = new_tail

    r_gate = _sigmoid(_diag_dot(xc, wax, 0, d_lru) + bax[:, 0:d_lru])
    i_gate = _sigmoid(_diag_dot(xc, wax, d_lru, d_lru) + bax[:, d_lru:2 * d_lru])
    neg_lam = -lam[...]
    softplus = jnp.maximum(neg_lam, 0.0) + jnp.log1p(jnp.exp(-jnp.abs(neg_lam)))
    log_a = (-LRU_C * r_gate) * softplus
    a_t = jnp.exp(log_a)
    la[...] = a_t
    lb[...] = jnp.sqrt(-jnp.tanh(log_a) * (a_t * a_t + 1.0)) * (i_gate * xc)

    def lru_step(t, h):
        r0 = pl.multiple_of(t * nb, nb)
        hn = la[pl.ds(r0, nb), :] * h + lb[pl.ds(r0, nb), :]
        lb[pl.ds(r0, nb), :] = hn
        return hn

    lruh[...] = lax.fori_loop(0, tl, lru_step, lruh[...], unroll=min(tl, 8))
    out_b = lb[...] * _gelu(proj[:, c1:c2])

    merged = (_sigmoid(proj[:, c2:c3]) * _dot(out_a, wpa[...])
              + _sigmoid(proj[:, c3:c3 + d_model]) * _dot(out_b, wpb[...]))
    x1_ref[...] = x + _dot(merged, wout[...])


def _mixer(x_in, s5re0, s5im0, lruh0, conv0, w, *, nb, tl, cw):
    batch_major = x_in.ndim == 3
    d_model = x_in.shape[-1]
    rows_total = x_in.shape[0] * x_in.shape[1] if batch_major else x_in.shape[0]
    rows = nb * tl
    n_state = s5re0.shape[-1]
    d_lru = lruh0.shape[-1]
    d_in = w["win"].shape[-1]
    tail = (CONV_W - 1) * nb
    weights = (w["nmix"], w["win"], w["bmat"], w["abre"], w["abim"], w["cre"], w["cim"],
               w["dskip"], w["wglu"], w["bglu"], w["convw"], w["convb"], w["wax"], w["bax"],
               w["lam"], w["wpa"], w["wpb"], w["wout"])
    states = (s5re0, s5im0, lruh0, conv0)
    row_spec = pl.BlockSpec((rows, d_model), lambda i: (i, 0))
    x_spec = pl.BlockSpec((nb, tl, d_model), lambda i: (0, i, 0)) if batch_major else row_spec
    slabs = [pltpu.VMEM((d_model // LANES, rows, LANES), F32)] if batch_major else []
    return pl.pallas_call(
        functools.partial(_mixer_body, nb=nb, tl=tl, cw=cw),
        grid=(rows_total // rows,),
        in_specs=[x_spec] + [_const_spec(a.shape) for a in states + weights],
        out_specs=[row_spec] + [_const_spec(a.shape) for a in states],
        out_shape=[jax.ShapeDtypeStruct((rows_total, d_model), F32)]
        + [jax.ShapeDtypeStruct(a.shape, F32) for a in states],
        scratch_shapes=[
            pltpu.VMEM((rows, d_in), F32),
            pltpu.VMEM((rows, 2 * n_state), F32),
            pltpu.VMEM((rows, d_lru), F32),
            pltpu.VMEM((rows, d_lru), F32),
            pltpu.VMEM((rows + tail, d_lru), F32),
        ] + slabs,
        compiler_params=pltpu.CompilerParams(
            dimension_semantics=("arbitrary",), vmem_limit_bytes=VMEM_LIMIT),
        name="mixer",
    )(x_in, *states, *weights)


def _peer_select_body(x1_ref, nffn, wq, keys, xn_ref, e1_ref, n1_ref, e2_ref, r2_ref,
                      vals, sc, cand, *, tt, heads, n_keys):
    neg_inf = -jnp.inf
    xn = _rms(x1_ref[...], nffn[...]).astype(BF16)
    xn_ref[...] = xn
    q = jnp.dot(xn, wq[...], preferred_element_type=F32).astype(BF16)
    dk = keys.shape[-1]
    rowv = lax.broadcasted_iota(jnp.int32, (TOPK, LANES), 0)

    def extract(loads):
        def rnd(k, carry):
            nxt = []
            for load, (m_prev, out) in zip(loads, carry):
                s = load()
                m = jnp.max(jnp.where(s < m_prev, s, neg_inf), axis=0, keepdims=True)
                nxt.append((m, jnp.where(rowv == k, m, out)))
            return tuple(nxt)
        init = tuple((jnp.full((1, LANES), jnp.inf, F32), jnp.full((TOPK, LANES), neg_inf, F32))
                     for _ in loads)
        return [out for _, out in lax.fori_loop(0, TOPK, rnd, init)]

    def extract_all(loads):
        outs = []
        for i in range(0, len(loads), EXTRACT_CHAINS):
            outs += extract(loads[i:i + EXTRACT_CHAINS])
        return outs

    for g in range(2 * heads):
        sc[g] = lax.dot_general(keys[g], q[:, g * dk:(g + 1) * dk], (((1,), (1,)), ((), ())),
                                preferred_element_type=F32)

    row8 = lax.broadcasted_iota(jnp.int32, (SUBLANES, LANES), 0)
    cols = [slice(c * LANES, (c + 1) * LANES) for c in range(tt // LANES)]
    for h in range(heads):
        hs = slice(h * n_keys, (h + 1) * n_keys)
        hp = slice(h * n_keys // 2, (h + 1) * n_keys // 2)
        groups = [(g, cs) for g in (2 * h, 2 * h + 1) for cs in cols]
        tops = extract_all([functools.partial(lambda g, cs: sc[g, :, cs], g, cs) for g, cs in groups])
        for (g, cs), top in zip(groups, tops):
            vals[g, :, cs] = top

        for cs in cols:
            a1 = vals[2 * h, :, cs]
            a2 = vals[2 * h + 1, :, cs]
            cand[0:TOPK, cs] = a1[0:1, :] + a2
            for p in range(1, TOPK):
                nq = TOPK // (p + 1)
                cand[TOPK + (p - 1) * SUBLANES:TOPK + p * SUBLANES, cs] = jnp.where(
                    row8 < nq, a1[p:p + 1, :] + a2[0:SUBLANES, :], neg_inf)
        sums = extract_all([functools.partial(lambda cs: cand[:, cs], cs) for cs in cols])

        for cs, top in zip(cols, sums):
            a1 = vals[2 * h, :, cs]
            a2 = vals[2 * h + 1, :, cs]
            zsum = jnp.sum(jnp.exp(top - top[0:1, :]), axis=0, keepdims=True)
            thr = top[TOPK - 1:TOPK, :]
            s1 = sc[2 * h, :, cs]
            s2 = sc[2 * h + 1, :, cs]
            cnt = jnp.zeros_like(s1)
            rank = jnp.full_like(s2, float(TOPK))
            for k in range(TOPK):
                n_k = jnp.sum(jnp.where(a1[k:k + 1, :] + a2 >= thr, 1.0, 0.0), axis=0, keepdims=True)
                cnt = jnp.where(s1 == a1[k:k + 1, :], n_k, cnt)
                rank = jnp.where(s2 == a2[k:k + 1, :], float(k), rank)
            e1_ref[hs, cs] = jnp.exp(s1 - a1[0:1, :]) / zsum
            n1_ref[hs, cs] = cnt
            e2_ref[hp, cs] = pltpu.bitcast(jnp.exp(s2 - a2[0:1, :]).astype(BF16), jnp.uint32)
            r2_ref[hp, cs] = pltpu.bitcast(rank.astype(BF16), jnp.uint32)


def _peer_select(x1, nffn, wq, keys, *, tt):
    t_total, d_model = x1.shape
    groups, n_keys, _ = keys.shape
    heads = groups // 2
    fac = pl.BlockSpec((heads * n_keys, tt), lambda i: (0, i))
    fac2 = pl.BlockSpec((heads * n_keys // 2, tt), lambda i: (0, i))
    return pl.pallas_call(
        functools.partial(_peer_select_body, tt=tt, heads=heads, n_keys=n_keys),
        grid=(t_total // tt,),
        in_specs=[pl.BlockSpec((tt, d_model), lambda i: (i, 0)),
                  _const_spec(nffn.shape), _const_spec(wq.shape), _const_spec(keys.shape)],
        out_specs=[pl.BlockSpec((tt, d_model), lambda i: (i, 0)), fac, fac, fac2, fac2],
        out_shape=[jax.ShapeDtypeStruct((t_total, d_model), BF16)]
        + [jax.ShapeDtypeStruct((heads * n_keys, t_total), F32)] * 2
        + [jax.ShapeDtypeStruct((heads * n_keys // 2, t_total), jnp.uint32)] * 2,
        scratch_shapes=[pltpu.VMEM((groups, TOPK, tt), F32),
                        pltpu.VMEM((groups, n_keys, tt), F32),
                        pltpu.VMEM((TOPK + (TOPK - 1) * SUBLANES, tt), F32)],
        compiler_params=pltpu.CompilerParams(
            dimension_semantics=("parallel",), vmem_limit_bytes=VMEM_LIMIT),
        name="peer_select",
    )(x1, nffn, wq, keys)


def _peer_dense_body(xn_ref, x1_ref, e1_ref, n1_ref, e2_ref, r2_ref, u_ref, vt_ref, nfin,
                     y_ref, acc, wgt, ys, *, tt, ni, heads, n_keys, chunk, nb):
    e = pl.program_id(1)

    @pl.when(e == 0)
    def _():
        acc[...] = jnp.zeros_like(acc)

    assert ni % SUBLANES == 0
    xn = xn_ref[...]
    rows_c = chunk * n_keys

    def first_key_row(ref, h, cs, i_loc):
        base = h * n_keys + e * ni + i_loc // SUBLANES * SUBLANES
        return ref[pl.ds(pl.multiple_of(base, SUBLANES), SUBLANES), cs][
            i_loc % SUBLANES:i_loc % SUBLANES + 1, :]

    for k in range(ni // chunk):
        r_lo = k * rows_c
        act = lax.dot_general(u_ref[r_lo:r_lo + rows_c, :], xn, (((1,), (1,)), ((), ())),
                              preferred_element_type=F32)
        for il in range(chunk):
            i_loc = k * chunk + il
            for c in range(tt // LANES):
                cs = slice(c * LANES, (c + 1) * LANES)
                bcast = lambda ref, h: jnp.broadcast_to(
                    first_key_row(ref, h, cs, i_loc), (BF16_ROWS, LANES)).astype(BF16)
                e1b = [bcast(e1_ref, h) for h in range(heads)]
                n1b = [bcast(n1_ref, h) for h in range(heads)]
                for rb in range(n_keys // BF16_ROWS):
                    r0 = rb * BF16_ROWS
                    gate = jnp.zeros((BF16_ROWS, LANES), BF16)
                    for h in range(heads):
                        js = slice((h * n_keys + r0) // 2, (h * n_keys + r0 + BF16_ROWS) // 2)
                        e2 = pltpu.bitcast(e2_ref[js, cs], BF16)
                        r2 = pltpu.bitcast(r2_ref[js, cs], BF16)
                        gate = gate + e1b[h] * jnp.minimum(e2, jnp.maximum(n1b[h] - r2, 0.0))
                    a = act[il * n_keys + r0:il * n_keys + r0 + BF16_ROWS, cs]
                    wgt[i_loc * n_keys + r0:i_loc * n_keys + r0 + BF16_ROWS, cs] = (
                        gate * _gelu_lowp(a))
        acc[...] += jnp.dot(vt_ref[:, r_lo:r_lo + rows_c], wgt[r_lo:r_lo + rows_c, :],
                            preferred_element_type=F32)

    @pl.when(e == pl.num_programs(1) - 1)
    def _():
        y = _rms(x1_ref[...] + acc[...].T, nfin[...])
        if nb is None:
            y_ref[...] = y
        else:
            for k in range(y.shape[-1] // LANES):
                ys[k] = y[:, k * LANES:(k + 1) * LANES]
            for b in range(nb):
                for k in range(y.shape[-1] // LANES):
                    y_ref[b, :, k * LANES:(k + 1) * LANES] = ys[k, pl.ds(b, tt // nb, stride=nb), :]


def _peer_dense(xn, x1, e1, n1, e2, r2, u_tab, vt_tab, nfin, *, tt, ni, heads, nb=None):
    t_total, d_model = x1.shape
    n_keys = e1.shape[0] // heads
    te = ni * n_keys
    n_experts = u_tab.shape[0]
    tok = lambda shape: pl.BlockSpec(shape, lambda t, e: (t, 0))
    fac = pl.BlockSpec((heads * n_keys, tt), lambda t, e: (0, t))
    fac2 = pl.BlockSpec((heads * n_keys // 2, tt), lambda t, e: (0, t))
    tab = pl.BlockSpec((te, d_model), lambda t, e: (e, 0))
    tab_t = pl.BlockSpec((d_model, te), lambda t, e: (0, e))
    return pl.pallas_call(
        functools.partial(_peer_dense_body, tt=tt, ni=ni, heads=heads, n_keys=n_keys, chunk=ni, nb=nb),
        grid=(t_total // tt, n_experts // te),
        in_specs=[tok((tt, d_model)), tok((tt, d_model)), fac, fac, fac2, fac2, tab, tab_t,
                  pl.BlockSpec(nfin.shape, lambda t, e: (0, 0))],
        out_specs=(tok((tt, d_model)) if nb is None else
                   pl.BlockSpec((nb, tt // nb, d_model), lambda t, e: (0, t, 0))),
        out_shape=jax.ShapeDtypeStruct(
            (t_total, d_model) if nb is None else (nb, t_total // nb, d_model), F32),
        scratch_shapes=[pltpu.VMEM((d_model, tt), F32),
                        pltpu.VMEM((te, tt), BF16),
                        pltpu.VMEM((d_model // LANES, tt, LANES), F32)],
        compiler_params=pltpu.CompilerParams(
            dimension_semantics=("parallel", "arbitrary"), vmem_limit_bytes=VMEM_LIMIT),
        name="peer_dense",
    )(xn, x1, e1, n1, e2, r2, u_tab, vt_tab, nfin)


def _layer(x_tm, s5re0, s5im0, lruh0, conv0, w, *, nb, tl, cw, tt, ni, batch_major_out=False):
    x1, s5re, s5im, lruh, convo = _mixer(x_tm, s5re0, s5im0, lruh0, conv0, w, nb=nb, tl=tl, cw=cw)
    xn, e1, n1, e2, r2 = _peer_select(x1, w["nffn"], w["wq"], w["keys"], tt=tt)
    y = _peer_dense(xn, x1, e1, n1, e2, r2, w["u"], w["vt"], w["nfin"], tt=tt, ni=ni,
                    heads=w["keys"].shape[0] // 2, nb=nb if batch_major_out else None)
    return y, s5re, s5im, lruh, convo


def kernel(x_prompt, x_sample, state_s5_re, state_s5_im, state_lru_h, state_conv, w_in, s5_lam_re, s5_lam_im, s5_log_dt, s5_b_re, s5_b_im, s5_c_re, s5_c_im, s5_d, s5_w_glu, s5_b_glu, conv_w, conv_b, lru_w_a, lru_b_a, lru_w_x, lru_b_x, lru_lam, w_proj_a, w_proj_b, w_out, norm_mix, norm_ffn, peer_w_q, peer_keys, peer_u, peer_v, norm_final):
    depth = w_in.shape[0]
    assert depth == 1, "single trunk layer"
    bp, seq, d_model = x_prompt.shape
    bs, dec_seq, _ = x_sample.shape
    assert dec_seq == 1
    groups, n_state_g = s5_lam_re.shape[1:]
    g_ch = s5_b_re.shape[-1]
    d_lru = lru_lam.shape[1] * lru_lam.shape[2]
    heads, _, n_keys, dk = peer_keys.shape[1:]
    n_state = groups * n_state_g
    row = lambda a: a.reshape(1, -1).astype(F32)

    abar_re, abar_im, bb_re, bb_im = _s5_discretise(
        s5_lam_re[0], s5_lam_im[0], s5_log_dt[0], s5_b_re[0], s5_b_im[0])
    bb = lambda a: _block_diag(a.reshape(groups, g_ch, n_state_g))
    cmat = lambda c: _block_diag(jnp.transpose(c, (0, 2, 1)))
    w = dict(
        nmix=row(norm_mix[0]), win=w_in[0].astype(BF16),
        bmat=jnp.concatenate([bb(bb_re), bb(bb_im)], axis=1).astype(BF16),
        abre=abar_re[::g_ch].reshape(1, n_state), abim=abar_im[::g_ch].reshape(1, n_state),
        cre=cmat(s5_c_re[0]).astype(BF16), cim=cmat(s5_c_im[0]).astype(BF16),
        dskip=row(s5_d[0]), wglu=s5_w_glu[0].astype(BF16), bglu=row(s5_b_glu[0]),
        convw=conv_w[0], convb=row(conv_b[0]),
        wax=jnp.concatenate([_block_diag(lru_w_a[0]), _block_diag(lru_w_x[0])], axis=1).astype(BF16),
        bax=jnp.concatenate([row(lru_b_a[0]), row(lru_b_x[0])], axis=1),
        lam=row(lru_lam[0]), wpa=w_proj_a[0].astype(BF16), wpb=w_proj_b[0].astype(BF16),
        wout=w_out[0].astype(BF16), nffn=row(norm_ffn[0]), wq=peer_w_q[0].astype(BF16),
        keys=peer_keys[0].reshape(2 * heads, n_keys, dk).astype(BF16),
        u=peer_u[0].astype(BF16), vt=peer_v[0].T.astype(BF16), nfin=row(norm_final),
    )

    zeros = lambda *s: jnp.zeros(s, F32)
    yp, p_re, p_im, p_h, p_conv = _layer(
        x_prompt, zeros(bp, n_state), zeros(bp, n_state), zeros(bp, d_lru),
        zeros((CONV_W - 1) * bp, d_lru), w, nb=bp, tl=64, cw=8 * LANES, tt=512, ni=2 * SUBLANES, batch_major_out=True)
    y_prompt = yp

    conv_tm = jnp.transpose(state_conv[0], (1, 0, 2)).reshape((CONV_W - 1) * bs, d_lru)
    ys, s_re, s_im, s_h, s_conv = _layer(
        x_sample.reshape(bs, d_model), state_s5_re[0].reshape(bs, n_state),
        state_s5_im[0].reshape(bs, n_state), state_lru_h[0], conv_tm, w,
        nb=bs, tl=1, cw=LANES, tt=LANES, ni=2 * SUBLANES)
    y_sample = ys.reshape(bs, 1, d_model)

    st = lambda a, b: a.reshape(1, b, groups, n_state_g)
    cv = lambda a, b: jnp.transpose(a.reshape(CONV_W - 1, b, d_lru), (1, 0, 2))[None]
    return (y_prompt, y_sample,
            st(p_re, bp), st(p_im, bp), p_h[None], cv(p_conv, bp),
            st(s_re, bs), st(s_im, bs), s_h[None], cv(s_conv, bs))
```

```python
import functools
import math

import numpy as np
import jax
import jax.numpy as jnp
from jax import lax
from jax.experimental import pallas as pl
from jax.experimental.pallas import tpu as pltpu

F32 = jnp.float32
BF16 = jnp.bfloat16

EPS = 1e-6
LRU_C = 8.0
CONV_W = 4
LANES = 128
SUBLANES = 8
BF16_ROWS = 16
TOPK = 16
EXTRACT_CHAINS = 8
DIAG_SPLIT = 2
VMEM_LIMIT = 56 * 1024 * 1024


def _gelu(x):
    c = math.sqrt(2.0 / math.pi)
    return x * (0.5 * (1.0 + jnp.tanh(c * (x + 0.044715 * (x * x * x)))))


def _bf16_pair(c):
    hi = float(np.asarray(c, dtype=BF16))
    return hi, float(np.asarray(c - hi, dtype=BF16))


def _gelu_lowp(x):
    c_hi, c_lo = _bf16_pair(math.sqrt(2.0 / math.pi))
    k_hi, k_lo = _bf16_pair(math.sqrt(2.0 / math.pi) * 0.044715)
    x = x.astype(BF16)
    x2 = x * x
    inner = x * (((k_hi * x2 + k_lo * x2) + c_hi) + c_lo)
    return x * (0.5 + 0.5 * jnp.tanh(inner))


def _sigmoid(x):
    return 1.0 / (1.0 + jnp.exp(-x))


def _rms(x, g):
    return x * lax.rsqrt(jnp.mean(x * x, axis=-1, keepdims=True) + EPS) * g


def _dot(a, b):
    return jnp.dot(a.astype(BF16), b, preferred_element_type=F32)


def _diag_dot(x, w_ref, col0, width, out_ref=None):
    k = x.shape[-1] // DIAG_SPLIT
    n = width // DIAG_SPLIT
    parts = []
    for b in range(DIAG_SPLIT):
        cols = slice(col0 + b * n, col0 + (b + 1) * n)
        part = _dot(x[:, b * k:(b + 1) * k], w_ref[b * k:(b + 1) * k, cols])
        if out_ref is None:
            parts.append(part)
        else:
            out_ref[:, cols] = part
    return None if out_ref is not None else jnp.concatenate(parts, axis=-1)


def _const_spec(shape):
    nd = len(shape)
    return pl.BlockSpec(shape, lambda *_: (0,) * nd, pipeline_mode=pl.Buffered(1))


def _s5_disc_body(lr_ref, li_ref, ldt_ref, br_ref, bi_ref,
                  are_ref, aim_ref, bbr_ref, bbi_ref):
    lr = lr_ref[...]
    li = li_ref[...]
    dt = jnp.exp(ldt_ref[...])
    mag = jnp.exp(lr * dt)
    a_re = mag * jnp.cos(li * dt)
    a_im = mag * jnp.sin(li * dt)
    nr = a_re - 1.0
    ni = a_im
    den = lr * lr + li * li
    f_re = (nr * lr + ni * li) / den
    f_im = (ni * lr - nr * li) / den
    br = br_ref[...]
    bi = bi_ref[...]
    are_ref[...] = a_re
    aim_ref[...] = a_im
    bbr_ref[...] = f_re * br - f_im * bi
    bbi_ref[...] = f_re * bi + f_im * br


def _s5_discretise(lam_re, lam_im, log_dt, b_re, b_im):
    g, p, h = b_re.shape
    rep = lambda a: jnp.repeat(a, h, axis=0)
    tr = lambda b: jnp.transpose(b, (0, 2, 1)).reshape(g * h, p)
    shp = jax.ShapeDtypeStruct((g * h, p), F32)
    return pl.pallas_call(
        _s5_disc_body,
        out_shape=(shp, shp, shp, shp),
        name="s5_discretise",
    )(rep(lam_re), rep(lam_im), rep(log_dt[:, None]), tr(b_re), tr(b_im))


def _block_diag(blocks):
    g, r, c = blocks.shape
    eye = jnp.eye(g, dtype=blocks.dtype)
    return (blocks[:, :, None, :] * eye[:, None, :, None]).reshape(g * r, g * c)


def _mixer_body(x_ref, s5re0, s5im0, lruh0, conv0,
                nmix, win, bmat, abre, abim, cre, cim, dskip, wglu, bglu,
                convw, convb, wax, bax, lam, wpa, wpb, wout,
                x1_ref, s5re, s5im, lruh, convo,
                proj, bu, la, lb, xbuf, xs=None, *, nb, tl, cw):
    rows = nb * tl
    d_s5 = dskip.shape[-1]
    d_lru = lam.shape[-1]
    d_model = x_ref.shape[-1]
    n_state = abre.shape[-1]
    tail = (CONV_W - 1) * nb

    @pl.when(pl.program_id(0) == 0)
    def _():
        s5re[...] = s5re0[...]
        s5im[...] = s5im0[...]
        lruh[...] = lruh0[...]
        xbuf[0:tail, :] = conv0[...]

    if xs is None:
        x = x_ref[...]
    else:
        for b in range(nb):
            for k in range(d_model // LANES):
                xs[k, pl.ds(b, tl, stride=nb), :] = x_ref[b, :, k * LANES:(k + 1) * LANES]
        x = jnp.concatenate([xs[k] for k in range(d_model // LANES)], axis=-1)
    proj[...] = _dot(_rms(x, nmix[...]), win[...])
    c0, c1, c2, c3 = d_s5, d_s5 + d_lru, d_s5 + 2 * d_lru, d_s5 + 2 * d_lru + d_model

    u = proj[:, 0:c0]
    _diag_dot(u, bmat, 0, n_state, out_ref=bu)
    _diag_dot(u, bmat, n_state, n_state, out_ref=bu)
    for c in range(n_state // cw):
        lo = c * cw
        ar = jnp.broadcast_to(abre[:, lo:lo + cw], (nb, cw))
        ai = jnp.broadcast_to(abim[:, lo:lo + cw], (nb, cw))

        def s5_step(t, carry, lo=lo, ar=ar, ai=ai):
            hr, hi = carry
            r0 = pl.multiple_of(t * nb, nb)
            br = bu[pl.ds(r0, nb), lo:lo + cw]
            bi = bu[pl.ds(r0, nb), n_state + lo:n_state + lo + cw]
            nr = ar * hr - ai * hi + br
            ni = ar * hi + ai * hr + bi
            bu[pl.ds(r0, nb), lo:lo + cw] = nr
            bu[pl.ds(r0, nb), n_state + lo:n_state + lo + cw] = ni
            return nr, ni

        hr, hi = lax.fori_loop(0, tl, s5_step, (s5re[:, lo:lo + cw], s5im[:, lo:lo + cw]),
                               unroll=min(tl, 8))
        s5re[:, lo:lo + cw] = hr
        s5im[:, lo:lo + cw] = hi

    y = (_diag_dot(bu[:, 0:n_state], cre, 0, d_s5)
         - _diag_dot(bu[:, n_state:2 * n_state], cim, 0, d_s5))
    y = y + dskip[...] * u
    z = _gelu(y)
    out_a = z * _sigmoid(_dot(z, wglu[...]) + bglu[...])

    xbuf[tail:tail + rows, :] = proj[:, c0:c1]
    acc = xbuf[0:rows, :] * convw[0:1, :]
    for k in range(1, CONV_W):
        acc = acc + xbuf[k * nb:k * nb + rows, :] * convw[k:k + 1, :]
    xc = convb[...] + acc
    new_tail = xbuf[tl * nb:tl * nb + tail, :]
    convo[...] = new_tail
    xbuf[0:tail, :] = new_tail

    r_gate = _sigmoid(_diag_dot(xc, wax, 0, d_lru) + bax[:, 0:d_lru])
    i_gate = _sigmoid(_diag_dot(xc, wax, d_lru, d_lru) + bax[:, d_lru:2 * d_lru])
    neg_lam = -lam[...]
    softplus = jnp.maximum(neg_lam, 0.0) + jnp.log1p(jnp.exp(-jnp.abs(neg_lam)))
    log_a = (-LRU_C * r_gate) * softplus
    a_t = jnp.exp(log_a)
    la[...] = a_t
    lb[...] = jnp.sqrt(-jnp.tanh(log_a) * (a_t * a_t + 1.0)) * (i_gate * xc)

    def lru_step(t, h):
        r0 = pl.multiple_of(t * nb, nb)
        hn = la[pl.ds(r0, nb), :] * h + lb[pl.ds(r0, nb), :]
        lb[pl.ds(r0, nb), :] = hn
        return hn

    lruh[...] = lax.fori_loop(0, tl, lru_step, lruh[...], unroll=min(tl, 8))
    out_b = lb[...] * _gelu(proj[:, c1:c2])

    merged = (_sigmoid(proj[:, c2:c3]) * _dot(out_a, wpa[...])
              + _sigmoid(proj[:, c3:c3 + d_model]) * _dot(out_b, wpb[...]))
    x1_ref[...] = x + _dot(merged, wout[...])


def _mixer(x_in, s5re0, s5im0, lruh0, conv0, w, *, nb, tl, cw):
    batch_major = x_in.ndim == 3
    d_model = x_in.shape[-1]
    rows_total = x_in.shape[0] * x_in.shape[1] if batch_major else x_in.shape[0]
    rows = nb * tl
    n_state = s5re0.shape[-1]
    d_lru = lruh0.shape[-1]
    d_in = w["win"].shape[-1]
    tail = (CONV_W - 1) * nb
    weights = (w["nmix"], w["win"], w["bmat"], w["abre"], w["abim"], w["cre"], w["cim"],
               w["dskip"], w["wglu"], w["bglu"], w["convw"], w["convb"], w["wax"], w["bax"],
               w["lam"], w["wpa"], w["wpb"], w["wout"])
    states = (s5re0, s5im0, lruh0, conv0)
    row_spec = pl.BlockSpec((rows, d_model), lambda i: (i, 0))
    x_spec = pl.BlockSpec((nb, tl, d_model), lambda i: (0, i, 0)) if batch_major else row_spec
    slabs = [pltpu.VMEM((d_model // LANES, rows, LANES), F32)] if batch_major else []
    return pl.pallas_call(
        functools.partial(_mixer_body, nb=nb, tl=tl, cw=cw),
        grid=(rows_total // rows,),
        in_specs=[x_spec] + [_const_spec(a.shape) for a in states + weights],
        out_specs=[row_spec] + [_const_spec(a.shape) for a in states],
        out_shape=[jax.ShapeDtypeStruct((rows_total, d_model), F32)]
        + [jax.ShapeDtypeStruct(a.shape, F32) for a in states],
        scratch_shapes=[
            pltpu.VMEM((rows, d_in), F32),
            pltpu.VMEM((rows, 2 * n_state), F32),
            pltpu.VMEM((rows, d_lru), F32),
            pltpu.VMEM((rows, d_lru), F32),
            pltpu.VMEM((rows + tail, d_lru), F32),
        ] + slabs,
        compiler_params=pltpu.CompilerParams(
            dimension_semantics=("arbitrary",), vmem_limit_bytes=VMEM_LIMIT),
        name="mixer",
    )(x_in, *states, *weights)


def _peer_select_body(x1_ref, nffn, wq, keys, xn_ref, e1_ref, n1_ref, e2_ref, r2_ref,
                      vals, sc, cand, *, tt, heads, n_keys):
    neg_inf = -jnp.inf
    xn = _rms(x1_ref[...], nffn[...]).astype(BF16)
    xn_ref[...] = xn
    q = jnp.dot(xn, wq[...], preferred_element_type=F32).astype(BF16)
    dk = keys.shape[-1]
    rowv = lax.broadcasted_iota(jnp.int32, (TOPK, LANES), 0)

    def extract(loads):
        def rnd(k, carry):
            nxt = []
            for load, (m_prev, out) in zip(loads, carry):
                s = load()
                m = jnp.max(jnp.where(s < m_prev, s, neg_inf), axis=0, keepdims=True)
                nxt.append((m, jnp.where(rowv == k, m, out)))
            return tuple(nxt)
        init = tuple((jnp.full((1, LANES), jnp.inf, F32), jnp.full((TOPK, LANES), neg_inf, F32))
                     for _ in loads)
        return [out for _, out in lax.fori_loop(0, TOPK, rnd, init)]

    def extract_all(loads):
        outs = []
        for i in range(0, len(loads), EXTRACT_CHAINS):
            outs += extract(loads[i:i + EXTRACT_CHAINS])
        return outs

    for g in range(2 * heads):
        sc[g] = lax.dot_general(keys[g], q[:, g * dk:(g + 1) * dk], (((1,), (1,)), ((), ())),
                                preferred_element_type=F32)

    row8 = lax.broadcasted_iota(jnp.int32, (SUBLANES, LANES), 0)
    cols = [slice(c * LANES, (c + 1) * LANES) for c in range(tt // LANES)]
    for h in range(heads):
        hs = slice(h * n_keys, (h + 1) * n_keys)
        hp = slice(h * n_keys // 2, (h + 1) * n_keys // 2)
        groups = [(g, cs) for g in (2 * h, 2 * h + 1) for cs in cols]
        tops = extract_all([functools.partial(lambda g, cs: sc[g, :, cs], g, cs) for g, cs in groups])
        for (g, cs), top in zip(groups, tops):
            vals[g, :, cs] = top

        for cs in cols:
            a1 = vals[2 * h, :, cs]
            a2 = vals[2 * h + 1, :, cs]
            cand[0:TOPK, cs] = a1[0:1, :] + a2
            for p in range(1, TOPK):
                nq = TOPK // (p + 1)
                cand[TOPK + (p - 1) * SUBLANES:TOPK + p * SUBLANES, cs] = jnp.where(
                    row8 < nq, a1[p:p + 1, :] + a2[0:SUBLANES, :], neg_inf)
        sums = extract_all([functools.partial(lambda cs: cand[:, cs], cs) for cs in cols])

        for cs, top in zip(cols, sums):
            a1 = vals[2 * h, :, cs]
            a2 = vals[2 * h + 1, :, cs]
            zsum = jnp.sum(jnp.exp(top - top[0:1, :]), axis=0, keepdims=True)
            thr = top[TOPK - 1:TOPK, :]
            s1 = sc[2 * h, :, cs]
            s2 = sc[2 * h + 1, :, cs]
            cnt = jnp.zeros_like(s1)
            rank = jnp.full_like(s2, float(TOPK))
            for k in range(TOPK):
                n_k = jnp.sum(jnp.where(a1[k:k + 1, :] + a2 >= thr, 1.0, 0.0), axis=0, keepdims=True)
                cnt = jnp.where(s1 == a1[k:k + 1, :], n_k, cnt)
                rank = jnp.where(s2 == a2[k:k + 1, :], float(k), rank)
            e1_ref[hs, cs] = jnp.exp(s1 - a1[0:1, :]) / zsum
            n1_ref[hs, cs] = cnt
            e2_ref[hp, cs] = pltpu.bitcast(jnp.exp(s2 - a2[0:1, :]).astype(BF16), jnp.uint32)
            r2_ref[hp, cs] = pltpu.bitcast(rank.astype(BF16), jnp.uint32)


def _peer_select(x1, nffn, wq, keys, *, tt):
    t_total, d_model = x1.shape
    groups, n_keys, _ = keys.shape
    heads = groups // 2
    fac = pl.BlockSpec((heads * n_keys, tt), lambda i: (0, i))
    fac2 = pl.BlockSpec((heads * n_keys // 2, tt), lambda i: (0, i))
    return pl.pallas_call(
        functools.partial(_peer_select_body, tt=tt, heads=heads, n_keys=n_keys),
        grid=(t_total // tt,),
        in_specs=[pl.BlockSpec((tt, d_model), lambda i: (i, 0)),
                  _const_spec(nffn.shape), _const_spec(wq.shape), _const_spec(keys.shape)],
        out_specs=[pl.BlockSpec((tt, d_model), lambda i: (i, 0)), fac, fac, fac2, fac2],
        out_shape=[jax.ShapeDtypeStruct((t_total, d_model), BF16)]
        + [jax.ShapeDtypeStruct((heads * n_keys, t_total), F32)] * 2
        + [jax.ShapeDtypeStruct((heads * n_keys // 2, t_total), jnp.uint32)] * 2,
        scratch_shapes=[pltpu.VMEM((groups, TOPK, tt), F32),
                        pltpu.VMEM((groups, n_keys, tt), F32),
                        pltpu.VMEM((TOPK + (TOPK - 1) * SUBLANES, tt), F32)],
        compiler_params=pltpu.CompilerParams(
            dimension_semantics=("parallel",), vmem_limit_bytes=VMEM_LIMIT),
        name="peer_select",
    )(x1, nffn, wq, keys)


def _peer_dense_body(xn_ref, x1_ref, e1_ref, n1_ref, e2_ref, r2_ref, u_ref, vt_ref, nfin,
                     y_ref, acc, wgt, ys, *, tt, ni, heads, n_keys, chunk, nb):
    e = pl.program_id(1)

    @pl.when(e == 0)
    def _():
        acc[...] = jnp.zeros_like(acc)

    assert ni % SUBLANES == 0
    xn = xn_ref[...]
    rows_c = chunk * n_keys

    def first_key_row(ref, h, cs, i_loc):
        base = h * n_keys + e * ni + i_loc // SUBLANES * SUBLANES
        return ref[pl.ds(pl.multiple_of(base, SUBLANES), SUBLANES), cs][
            i_loc % SUBLANES:i_loc % SUBLANES + 1, :]

    for k in range(ni // chunk):
        r_lo = k * rows_c
        act = lax.dot_general(u_ref[r_lo:r_lo + rows_c, :], xn, (((1,), (1,)), ((), ())),
                              preferred_element_type=F32)
        for il in range(chunk):
            i_loc = k * chunk + il
            for c in range(tt // LANES):
                cs = slice(c * LANES, (c + 1) * LANES)
                bcast = lambda ref, h: jnp.broadcast_to(
                    first_key_row(ref, h, cs, i_loc), (BF16_ROWS, LANES)).astype(BF16)
                e1b = [bcast(e1_ref, h) for h in range(heads)]
                n1b = [bcast(n1_ref, h) for h in range(heads)]
                for rb in range(n_keys // BF16_ROWS):
                    r0 = rb * BF16_ROWS
                    gate = None
                    for h in range(heads):
                        js = slice((h * n_keys + r0) // 2, (h * n_keys + r0 + BF16_ROWS) // 2)
                        e2 = pltpu.bitcast(e2_ref[js, cs], BF16)
                        r2 = pltpu.bitcast(r2_ref[js, cs], BF16)
                        term = e1b[h] * jnp.minimum(e2, jnp.maximum(n1b[h] - r2, 0.0))
                        gate = term if gate is None else gate + term
                    a = act[il * n_keys + r0:il * n_keys + r0 + BF16_ROWS, cs]
                    wgt[i_loc * n_keys + r0:i_loc * n_keys + r0 + BF16_ROWS, cs] = (
                        gate * _gelu_lowp(a))
        acc[...] += jnp.dot(vt_ref[:, r_lo:r_lo + rows_c], wgt[r_lo:r_lo + rows_c, :],
                            preferred_element_type=F32)

    @pl.when(e == pl.num_programs(1) - 1)
    def _():
        y = _rms(x1_ref[...] + acc[...].T, nfin[...])
        if nb is None:
            y_ref[...] = y
        else:
            for k in range(y.shape[-1] // LANES):
                ys[k] = y[:, k * LANES:(k + 1) * LANES]
            for b in range(nb):
                for k in range(y.shape[-1] // LANES):
                    y_ref[b, :, k * LANES:(k + 1) * LANES] = ys[k, pl.ds(b, tt // nb, stride=nb), :]


def _peer_dense(xn, x1, e1, n1, e2, r2, u_tab, vt_tab, nfin, *, tt, ni, heads, nb=None):
    t_total, d_model = x1.shape
    n_keys = e1.shape[0] // heads
    te = ni * n_keys
    n_experts = u_tab.shape[0]
    tok = lambda shape: pl.BlockSpec(shape, lambda t, e: (t, 0))
    fac = pl.BlockSpec((heads * n_keys, tt), lambda t, e: (0, t))
    fac2 = pl.BlockSpec((heads * n_keys // 2, tt), lambda t, e: (0, t))
    tab = pl.BlockSpec((te, d_model), lambda t, e: (e, 0))
    tab_t = pl.BlockSpec((d_model, te), lambda t, e: (0, e))
    return pl.pallas_call(
        functools.partial(_peer_dense_body, tt=tt, ni=ni, heads=heads, n_keys=n_keys, chunk=ni, nb=nb),
        grid=(t_total // tt, n_experts // te),
        in_specs=[tok((tt, d_model)), tok((tt, d_model)), fac, fac, fac2, fac2, tab, tab_t,
                  pl.BlockSpec(nfin.shape, lambda t, e: (0, 0))],
        out_specs=(tok((tt, d_model)) if nb is None else
                   pl.BlockSpec((nb, tt // nb, d_model), lambda t, e: (0, t, 0))),
        out_shape=jax.ShapeDtypeStruct(
            (t_total, d_model) if nb is None else (nb, t_total // nb, d_model), F32),
        scratch_shapes=[pltpu.VMEM((d_model, tt), F32),
                        pltpu.VMEM((te, tt), BF16),
                        pltpu.VMEM((d_model // LANES, tt, LANES), F32)],
        compiler_params=pltpu.CompilerParams(
            dimension_semantics=("parallel", "arbitrary"), vmem_limit_bytes=VMEM_LIMIT),
        name="peer_dense",
    )(xn, x1, e1, n1, e2, r2, u_tab, vt_tab, nfin)


def _layer(x_tm, s5re0, s5im0, lruh0, conv0, w, *, nb, tl, cw, tt, ni, batch_major_out=False):
    x1, s5re, s5im, lruh, convo = _mixer(x_tm, s5re0, s5im0, lruh0, conv0, w, nb=nb, tl=tl, cw=cw)
    xn, e1, n1, e2, r2 = _peer_select(x1, w["nffn"], w["wq"], w["keys"], tt=tt)
    y = _peer_dense(xn, x1, e1, n1, e2, r2, w["u"], w["vt"], w["nfin"], tt=tt, ni=ni,
                    heads=w["keys"].shape[0] // 2, nb=nb if batch_major_out else None)
    return y, s5re, s5im, lruh, convo


def kernel(x_prompt, x_sample, state_s5_re, state_s5_im, state_lru_h, state_conv, w_in, s5_lam_re, s5_lam_im, s5_log_dt, s5_b_re, s5_b_im, s5_c_re, s5_c_im, s5_d, s5_w_glu, s5_b_glu, conv_w, conv_b, lru_w_a, lru_b_a, lru_w_x, lru_b_x, lru_lam, w_proj_a, w_proj_b, w_out, norm_mix, norm_ffn, peer_w_q, peer_keys, peer_u, peer_v, norm_final):
    depth = w_in.shape[0]
    assert depth == 1, "single trunk layer"
    bp, seq, d_model = x_prompt.shape
    bs, dec_seq, _ = x_sample.shape
    assert dec_seq == 1
    groups, n_state_g = s5_lam_re.shape[1:]
    g_ch = s5_b_re.shape[-1]
    d_lru = lru_lam.shape[1] * lru_lam.shape[2]
    heads, _, n_keys, dk = peer_keys.shape[1:]
    n_state = groups * n_state_g
    row = lambda a: a.reshape(1, -1).astype(F32)

    abar_re, abar_im, bb_re, bb_im = _s5_discretise(
        s5_lam_re[0], s5_lam_im[0], s5_log_dt[0], s5_b_re[0], s5_b_im[0])
    bb = lambda a: _block_diag(a.reshape(groups, g_ch, n_state_g))
    cmat = lambda c: _block_diag(jnp.transpose(c, (0, 2, 1)))
    w = dict(
        nmix=row(norm_mix[0]), win=w_in[0].astype(BF16),
        bmat=jnp.concatenate([bb(bb_re), bb(bb_im)], axis=1).astype(BF16),
        abre=abar_re[::g_ch].reshape(1, n_state), abim=abar_im[::g_ch].reshape(1, n_state),
        cre=cmat(s5_c_re[0]).astype(BF16), cim=cmat(s5_c_im[0]).astype(BF16),
        dskip=row(s5_d[0]), wglu=s5_w_glu[0].astype(BF16), bglu=row(s5_b_glu[0]),
        convw=conv_w[0], convb=row(conv_b[0]),
        wax=jnp.concatenate([_block_diag(lru_w_a[0]), _block_diag(lru_w_x[0])], axis=1).astype(BF16),
        bax=jnp.concatenate([row(lru_b_a[0]), row(lru_b_x[0])], axis=1),
        lam=row(lru_lam[0]), wpa=w_proj_a[0].astype(BF16), wpb=w_proj_b[0].astype(BF16),
        wout=w_out[0].astype(BF16), nffn=row(norm_ffn[0]), wq=peer_w_q[0].astype(BF16),
        keys=peer_keys[0].reshape(2 * heads, n_keys, dk).astype(BF16),
        u=peer_u[0].astype(BF16), vt=peer_v[0].T.astype(BF16), nfin=row(norm_final),
    )

    zeros = lambda *s: jnp.zeros(s, F32)
    yp, p_re, p_im, p_h, p_conv = _layer(
        x_prompt, zeros(bp, n_state), zeros(bp, n_state), zeros(bp, d_lru),
        zeros((CONV_W - 1) * bp, d_lru), w, nb=bp, tl=64, cw=8 * LANES, tt=512, ni=2 * SUBLANES, batch_major_out=True)
    y_prompt = yp

    conv_tm = jnp.transpose(state_conv[0], (1, 0, 2)).reshape((CONV_W - 1) * bs, d_lru)
    ys, s_re, s_im, s_h, s_conv = _layer(
        x_sample.reshape(bs, d_model), state_s5_re[0].reshape(bs, n_state),
        state_s5_im[0].reshape(bs, n_state), state_lru_h[0], conv_tm, w,
        nb=bs, tl=1, cw=LANES, tt=LANES, ni=2 * SUBLANES)
    y_sample = ys.reshape(bs, 1, d_model)

    st = lambda a, b: a.reshape(1, b, groups, n_state_g)
    cv = lambda a, b: jnp.transpose(a.reshape(CONV_W - 1, b, d_lru), (1, 0, 2))[None]
    return (y_prompt, y_sample,
            st(p_re, bp), st(p_im, bp), p_h[None], cv(p_conv, bp),
            st(s_re, bs), st(s_im, bs), s_h[None], cv(s_conv, bs))
```
